```python
import jax, jax.numpy as jnp
from jax import lax
import numpy as np

D_MODEL = 1024
BATCH = 8
SEQ = 2048
DEPTH = 4

N_MIXERS = 2
N_RG = (DEPTH + 1) // 2
N_HG = DEPTH // 2
LRU_WIDTH = D_MODEL
RG_HEADS = 8
RG_BLOCK = LRU_WIDTH // RG_HEADS
CONV_WIDTH = 4
LRU_C = 8.0
HG_HEADS = 8
HG_DK = D_MODEL // HG_HEADS
HG_DV = D_MODEL // HG_HEADS
HG_CHUNK = 64
D_FF = 4 * D_MODEL
NORM_EPS = 1e-6
GNORM_EPS = 1e-5

kernel_name = "hybrid_rglru_hgrn2_adaln_trunk"


def rms_norm(x, gain, eps=NORM_EPS):
    xf = x.astype(jnp.float32)
    y = xf * lax.rsqrt(jnp.mean(xf * xf, axis=-1, keepdims=True) + eps)
    return (y * gain.astype(jnp.float32)).astype(x.dtype)


def causal_depthwise_conv(x, w, b):
    T = x.shape[1]
    xp = jnp.pad(x, ((0, 0), (CONV_WIDTH - 1, 0), (0, 0)))
    y = b
    for tap in range(CONV_WIDTH):
        y = y + xp[:, tap:tap + T, :] * w[tap]
    return y


def rg_lru(x, w_a, b_a, w_x, b_x, lam):
    B, T, _ = x.shape
    xh = x.reshape(B, T, RG_HEADS, RG_BLOCK)
    r = jax.nn.sigmoid(jnp.einsum('bthi,hij->bthj', xh, w_a).reshape(B, T, LRU_WIDTH) + b_a)
    i = jax.nn.sigmoid(jnp.einsum('bthi,hij->bthj', xh, w_x).reshape(B, T, LRU_WIDTH) + b_x)
    log_a = -LRU_C * r.astype(jnp.float32) * jax.nn.softplus(-lam.astype(jnp.float32))
    a = jnp.exp(log_a)
    mult = jnp.sqrt(-jnp.expm1(2.0 * log_a))
    mult = jnp.where(jnp.arange(T)[None, :, None] == 0, 1.0, mult)
    u = mult * (i * x).astype(jnp.float32)

    def combine(left, right):
        a1, b1 = left
        a2, b2 = right
        return a1 * a2, a2 * b1 + b2

    _, h = lax.associative_scan(combine, (a, u), axis=1)
    return h.astype(x.dtype)


def rglru_block(h, w_in, conv_w, conv_b, w_a, b_a, w_x, b_x, lam, w_out):
    xb, yb = jnp.split(h @ w_in, 2, axis=-1)
    gate = jax.nn.gelu(yb, approximate=True)
    xb = causal_depthwise_conv(xb, conv_w, conv_b)
    hr = rg_lru(xb, w_a, b_a, w_x, b_x, lam)
    return (hr * gate) @ w_out


def hgrn2_chunk_scan(q, k, v, log_f):
    B, T, H, _ = q.shape
    N = T // HG_CHUNK

    def to_chunks(t):
        return t.astype(jnp.float32).reshape(B, N, HG_CHUNK, H, -1).transpose(1, 0, 3, 2, 4)

    qc, kc, vc, lc = to_chunks(q), to_chunks(k), to_chunks(v), to_chunks(log_f)
    bc = jnp.cumsum(lc, axis=3)
    causal = jnp.tril(jnp.ones((HG_CHUNK, HG_CHUNK), dtype=bool))

    def step(S, inp):
        q_, k_, v_, b_ = inp
        diff = b_[:, :, :, None, :] - b_[:, :, None, :, :]
        decay = jnp.exp(jnp.where(causal[:, :, None], diff, -jnp.inf))
        A = jnp.einsum('bhtsk,bhsk->bhts', q_[:, :, :, None, :] * decay, k_)
        o = A @ v_ + jnp.einsum('bhtk,bhkv->bhtv', q_ * jnp.exp(b_), S)
        b_last = b_[:, :, -1:, :]
        S = jnp.exp(b_last)[:, :, 0, :, None] * S + jnp.einsum(
            'bhsk,bhsv->bhkv', k_ * jnp.exp(b_last - b_), v_)
        return S, o

    S0 = jnp.zeros((B, H, HG_DK, HG_DV), jnp.float32)
    _, o = lax.scan(step, S0, (qc, kc, vc, bc))
    return o.transpose(1, 0, 3, 2, 4).reshape(B, T, H, HG_DV)


def hgrn2_block(h, w_in, lb, gnorm_w, w_out):
    B, T, _ = h.shape
    q, f, i, g = jnp.split(h @ w_in, 4, axis=-1)
    q = jax.nn.silu(q)
    fg = lb + (1.0 - lb) * jax.nn.sigmoid(f.astype(jnp.float32))
    log_f = jnp.log(fg)
    k = 1.0 - fg

    def heads(t):
        return t.reshape(B, T, HG_HEADS, -1)

    o = hgrn2_chunk_scan(heads(q), heads(k), heads(i), heads(log_f))
    o = rms_norm(o, gnorm_w, GNORM_EPS) * jax.nn.silu(heads(g).astype(jnp.float32))
    return (o.reshape(B, T, D_MODEL) @ w_out).astype(h.dtype)


def setup_inputs(seed: int = 0) -> dict:
    key = jax.random.key(seed)
    ks = jax.random.split(key, 24)
    f32 = jnp.float32
    nrm = lambda k, shape, s: jax.random.normal(k, shape, f32) * s
    u = jax.random.uniform(ks[11], (N_RG, LRU_WIDTH), f32, 0.9, 0.999)
    s = u ** (1.0 / LRU_C)
    lam = jnp.log(s) - jnp.log1p(-s)
    return {
        "x": nrm(ks[0], (BATCH, SEQ, D_MODEL), 1.0),
        "c": nrm(ks[1], (BATCH, D_MODEL), 1.0),
        "mod_w": nrm(ks[2], (DEPTH, D_MODEL, 6 * D_MODEL), D_MODEL ** -0.5),
        "mod_b": nrm(ks[3], (DEPTH, 6 * D_MODEL), 0.02),
        "norm_mix": 1.0 + nrm(ks[4], (DEPTH, D_MODEL), 0.02),
        "norm_mlp": 1.0 + nrm(ks[5], (DEPTH, D_MODEL), 0.02),
        "norm_final": 1.0 + nrm(ks[6], (D_MODEL,), 0.02),
        "rg_w_in": nrm(ks[7], (N_RG, D_MODEL, 2 * LRU_WIDTH), D_MODEL ** -0.5),
        "rg_conv_w": nrm(ks[8], (N_RG, CONV_WIDTH, LRU_WIDTH), CONV_WIDTH ** -0.5),
        "rg_conv_b": nrm(ks[9], (N_RG, LRU_WIDTH), 0.02),
        "rg_w_a": nrm(ks[10], (N_RG, RG_HEADS, RG_BLOCK, RG_BLOCK), RG_BLOCK ** -0.5),
        "rg_b_a": nrm(ks[12], (N_RG, LRU_WIDTH), 0.1),
        "rg_w_x": nrm(ks[13], (N_RG, RG_HEADS, RG_BLOCK, RG_BLOCK), RG_BLOCK ** -0.5),
        "rg_b_x": nrm(ks[14], (N_RG, LRU_WIDTH), 0.1),
        "rg_lambda": lam,
        "rg_w_out": nrm(ks[15], (N_RG, LRU_WIDTH, D_MODEL), LRU_WIDTH ** -0.5),
        "hg_w_in": nrm(ks[16], (N_HG, D_MODEL, 4 * D_MODEL), D_MODEL ** -0.5),
        "hg_lower_bounds": nrm(ks[17], (DEPTH, D_MODEL), 0.1),
        "hg_gnorm": 1.0 + nrm(ks[18], (N_HG, HG_DV), 0.02),
        "hg_w_out": nrm(ks[19], (N_HG, D_MODEL, D_MODEL), D_MODEL ** -0.5),
        "mlp_w1": nrm(ks[20], (DEPTH, D_MODEL, D_FF), D_MODEL ** -0.5),
        "mlp_w2": nrm(ks[21], (DEPTH, D_FF, D_MODEL), D_FF ** -0.5),
    }


def reference(x, c, mod_w, mod_b, norm_mix, norm_mlp, norm_final,
              rg_w_in, rg_conv_w, rg_conv_b, rg_w_a, rg_b_a, rg_w_x, rg_b_x, rg_lambda, rg_w_out,
              hg_w_in, hg_lower_bounds, hg_gnorm, hg_w_out, mlp_w1, mlp_w2):
    cs = jax.nn.silu(c)
    lb_all = jnp.cumsum(jax.nn.softmax(hg_lower_bounds.astype(jnp.float32), axis=0), axis=0)
    lb_all = lb_all - lb_all[0]
    for layer in range(DEPTH):
        mod = cs @ mod_w[layer] + mod_b[layer]
        sh_t, sc_t, g_t, sh_c, sc_c, g_c = [m[:, None, :] for m in jnp.split(mod, 6, axis=-1)]
        h = rms_norm(x, norm_mix[layer]) * (1.0 + sc_t) + sh_t
        j = layer // N_MIXERS
        if layer % N_MIXERS == 0:
            y = rglru_block(h, rg_w_in[j], rg_conv_w[j], rg_conv_b[j], rg_w_a[j], rg_b_a[j],
                            rg_w_x[j], rg_b_x[j], rg_lambda[j], rg_w_out[j])
        else:
            y = hgrn2_block(h, hg_w_in[j], lb_all[layer], hg_gnorm[j], hg_w_out[j])
        x = (x + g_t * y).astype(x.dtype)
        h = rms_norm(x, norm_mlp[layer]) * (1.0 + sc_c) + sh_c
        ff = jnp.square(jax.nn.relu(h @ mlp_w1[layer])) @ mlp_w2[layer]
        x = (x + g_c * ff).astype(x.dtype)
    return rms_norm(x, norm_final)
```

```python
import functools

import jax
import jax.numpy as jnp
from jax import lax
from jax.experimental import pallas as pl
from jax.experimental.pallas import tpu as pltpu

NORM_EPS = 1e-6
GNORM_EPS = 1e-5
LRU_C = 8.0
N_HEADS = 8
HEAD_DIM = 128
SUBLANES = 8
HG_CHUNK = 128
HG_DIAG = 16
VMEM_LIMIT = 56 * 1024 * 1024

_BF16 = jnp.bfloat16
_F32 = jnp.float32


def _dot(a, b):
    return jnp.dot(a, b, preferred_element_type=_F32)


def _sigmoid(x):
    return 1.0 / (1.0 + jnp.exp(-x))


def _silu(x):
    return x * _sigmoid(x)


def _gelu_tanh(x):
    return 0.5 * x * (1.0 + jnp.tanh(0.7978845608028654 * (x + 0.044715 * (x * x * x))))


def _rms_norm(x, gain, eps):
    ms = jnp.mean(x * x, axis=-1, keepdims=True)
    return x * lax.rsqrt(ms + eps) * gain


def _norm_mod(x, gain, scale, shift):
    return _rms_norm(x, gain, NORM_EPS) * (1.0 + scale) + shift


def _mod_kernel(c_ref, w_ref, b_ref, o_ref):
    cs = _silu(c_ref[...]).astype(_BF16)
    o_ref[...] = _dot(cs, w_ref[...].astype(_BF16)) + b_ref[...]


def _mod_call(c, mod_w, mod_b):
    depth, d, n6 = mod_w.shape
    bsz = c.shape[0]
    tn = 2048
    return pl.pallas_call(
        _mod_kernel,
        grid=(depth, n6 // tn),
        in_specs=[
            pl.BlockSpec((bsz, d), lambda l, j: (0, 0)),
            pl.BlockSpec((None, d, tn), lambda l, j: (l, 0, j)),
            pl.BlockSpec((None, 1, tn), lambda l, j: (l, 0, j)),
        ],
        out_specs=pl.BlockSpec((None, bsz, tn), lambda l, j: (l, 0, j)),
        out_shape=jax.ShapeDtypeStruct((depth, bsz, n6), _F32),
        compiler_params=pltpu.CompilerParams(
            dimension_semantics=("arbitrary", "arbitrary"),
            vmem_limit_bytes=VMEM_LIMIT),
        name="adaln_mod",
    )(c, mod_w, mod_b.reshape(depth, 1, n6))


def _mlp_kernel(x_ref, mod_ref, gain_ref, w1_ref, w2_ref, gf_ref, o_ref, hid_ref, *, final, fc):
    x = x_ref[...]
    m = mod_ref[...]
    h = _norm_mod(x, gain_ref[...], m[4:5], m[3:4]).astype(_BF16)
    d_ff = w1_ref.shape[1]
    for c in range(d_ff // fc):
        t = jnp.maximum(_dot(h, w1_ref[:, c * fc:(c + 1) * fc]), 0.0)
        hid_ref[:, c * fc:(c + 1) * fc] = (t * t).astype(_BF16)
    y = _dot(hid_ref[...], w2_ref[...])
    xn = x + m[5:6] * y
    if final:
        xn = _rms_norm(xn, gf_ref[...], NORM_EPS)
    o_ref[...] = xn


def _mlp_call(x, mod_l, gain, w1, w2, gain_final, *, final, tm=512, fc=1024):
    bsz, t, d = x.shape
    d_ff = w1.shape[1]
    nt = t // tm
    const = lambda b, i: (0, 0)
    return pl.pallas_call(
        functools.partial(_mlp_kernel, final=final, fc=fc),
        grid=(bsz, nt),
        in_specs=[
            pl.BlockSpec((None, tm, d), lambda b, i: (b, i, 0)),
            pl.BlockSpec((None, 6, d), lambda b, i: (b, 0, 0)),
            pl.BlockSpec((1, d), const),
            pl.BlockSpec((d, d_ff), const, pipeline_mode=pl.Buffered(1)),
            pl.BlockSpec((d_ff, d), const, pipeline_mode=pl.Buffered(1)),
            pl.BlockSpec((1, d), const),
        ],
        out_specs=pl.BlockSpec((None, tm, d), lambda b, i: (b, i, 0)),
        out_shape=jax.ShapeDtypeStruct((bsz, t, d), _F32),
        scratch_shapes=[pltpu.VMEM((tm, d_ff), _BF16)],
        compiler_params=pltpu.CompilerParams(
            dimension_semantics=("arbitrary", "arbitrary"),
            vmem_limit_bytes=VMEM_LIMIT),
        name="mlp_final" if final else "mlp",
    )(x, mod_l, gain, w1, w2, gain_final)


def _rg_kernel(x_ref, mod_ref, gain_ref, win_ref, cw_ref, cb_ref, wg_ref, ba_ref, bx_ref,
               lam_ref, wout_ref, o_ref, ext_ref, a_ref, u_ref, gate_ref, hc_ref):
    tb, d = x_ref.shape
    tblk = pl.program_id(1)

    @pl.when(tblk == 0)
    def _():
        ext_ref[0:SUBLANES, :] = jnp.zeros((SUBLANES, d), _F32)
        hc_ref[...] = jnp.zeros_like(hc_ref)

    m = mod_ref[...]
    h = _norm_mod(x_ref[...], gain_ref[...], m[1:2], m[0:1]).astype(_BF16)
    gate_ref[...] = _gelu_tanh(_dot(h, win_ref[:, d:]))
    ext_ref[SUBLANES:, :] = _dot(h, win_ref[:, :d])

    cw = cw_ref[...]
    n_tap = cw.shape[0]
    xc = cb_ref[...]
    for tap in range(n_tap):
        off = SUBLANES - (n_tap - 1) + tap
        xc = xc + ext_ref[off:off + tb, :] * cw[tap:tap + 1, :]
    ext_ref[0:SUBLANES, :] = ext_ref[tb:tb + SUBLANES, :]

    lam = lam_ref[...]
    nl = -lam
    softplus = jnp.maximum(nl, 0.0) + jnp.log1p(jnp.exp(-jnp.abs(nl)))
    first_row = jnp.logical_and(
        lax.broadcasted_iota(jnp.int32, (tb, HEAD_DIM), 0) == 0, tblk == 0)
    for hd in range(N_HEADS):
        sl = slice(hd * HEAD_DIM, (hd + 1) * HEAD_DIM)
        xh = xc[:, sl]
        gp = _dot(xh.astype(_BF16), wg_ref[hd])
        r = _sigmoid(gp[:, :HEAD_DIM] + ba_ref[:, sl])
        i = _sigmoid(gp[:, HEAD_DIM:] + bx_ref[:, sl])
        log_a = (-LRU_C) * r * softplus[:, sl]
        a = jnp.exp(log_a)
        mult = jnp.sqrt(-jnp.tanh(log_a) * (1.0 + a * a))
        mult = jnp.where(first_row, 1.0, mult)
        a_ref[:, sl] = a
        u_ref[:, sl] = mult * (i * xh)

    sub = lax.broadcasted_iota(jnp.int32, (SUBLANES, d), 0)

    def tile_step(j, carry):
        r0 = pl.multiple_of(j * SUBLANES, SUBLANES)
        a = a_ref[pl.ds(r0, SUBLANES), :]
        u = u_ref[pl.ds(r0, SUBLANES), :]
        for shift in (1, 2, 4):
            keep = sub >= shift
            u = u + a * jnp.where(keep, pltpu.roll(u, shift, 0), 0.0)
            a = a * jnp.where(keep, pltpu.roll(a, shift, 0), 1.0)
        hh = u + a * carry
        u_ref[pl.ds(r0, SUBLANES), :] = hh
        return jnp.broadcast_to(hh[SUBLANES - 1:SUBLANES, :], (SUBLANES, d))

    hc_ref[...] = lax.fori_loop(0, tb // SUBLANES, tile_step, hc_ref[...], unroll=2)

    z = (u_ref[...] * gate_ref[...]).astype(_BF16)
    o_ref[...] = x_ref[...] + m[2:3] * _dot(z, wout_ref[...])


def _rg_call(x, mod_l, gain, w_in, conv_w, conv_b, wg, b_a, b_x, lam, w_out, *, tb=512):
    bsz, t, d = x.shape
    const2 = lambda b, i: (0, 0)
    const3 = lambda b, i: (0, 0, 0)
    return pl.pallas_call(
        _rg_kernel,
        grid=(bsz, t // tb),
        in_specs=[
            pl.BlockSpec((None, tb, d), lambda b, i: (b, i, 0)),
            pl.BlockSpec((None, 6, d), lambda b, i: (b, 0, 0)),
            pl.BlockSpec((1, d), const2),
            pl.BlockSpec(w_in.shape, const2, pipeline_mode=pl.Buffered(1)),
            pl.BlockSpec(conv_w.shape, const2),
            pl.BlockSpec((1, d), const2),
            pl.BlockSpec(wg.shape, const3, pipeline_mode=pl.Buffered(1)),
            pl.BlockSpec((1, d), const2),
            pl.BlockSpec((1, d), const2),
            pl.BlockSpec((1, d), const2),
            pl.BlockSpec(w_out.shape, const2, pipeline_mode=pl.Buffered(1)),
        ],
        out_specs=pl.BlockSpec((None, tb, d), lambda b, i: (b, i, 0)),
        out_shape=jax.ShapeDtypeStruct((bsz, t, d), _F32),
        scratch_shapes=[
            pltpu.VMEM((tb + SUBLANES, d), _F32),
            pltpu.VMEM((tb, d), _F32),
            pltpu.VMEM((tb, d), _F32),
            pltpu.VMEM((tb, d), _F32),
            pltpu.VMEM((SUBLANES, d), _F32),
        ],
        compiler_params=pltpu.CompilerParams(
            dimension_semantics=("arbitrary", "arbitrary"),
            vmem_limit_bytes=VMEM_LIMIT),
        name="rglru",
    )(x, mod_l, gain, w_in, conv_w, conv_b, wg, b_a, b_x, lam, w_out)


def _block_row_bcast(b, block, row):
    n, d = b.shape
    parts = []
    for s in range(0, n, block):
        parts.append(jnp.broadcast_to(b[s + row:s + row + 1, :], (block, d)))
    return jnp.concatenate(parts, axis=0)


def _hg_kernel(x_ref, mod_ref, gain_ref, win_ref, lbp_ref, gn_ref, wout_ref, o_ref,
               q_ref, k_ref, b_ref, v_ref, g_ref, oh_ref, st_ref, *, layer):
    tb, d = x_ref.shape
    cc = HG_CHUNK
    tblk = pl.program_id(1)

    @pl.when(tblk == 0)
    def _():
        st_ref[...] = jnp.zeros_like(st_ref)

    lbp = lbp_ref[...]
    depth = lbp.shape[0]
    mx = lbp[0:1]
    for j in range(1, depth):
        mx = jnp.maximum(mx, lbp[j:j + 1])
    es = [jnp.exp(lbp[j:j + 1] - mx) for j in range(depth)]
    tot = es[0]
    for j in range(1, depth):
        tot = tot + es[j]
    num = es[1]
    for j in range(2, layer + 1):
        num = num + es[j]
    lb = num / tot

    m = mod_ref[...]
    h = _norm_mod(x_ref[...], gain_ref[...], m[1:2], m[0:1]).astype(_BF16)
    q_ref[...] = _silu(_dot(h, win_ref[:, 0:d]))
    fg = lb + (1.0 - lb) * _sigmoid(_dot(h, win_ref[:, d:2 * d]))
    k_ref[...] = 1.0 - fg
    logf = jnp.log(fg)
    v_ref[...] = _dot(h, win_ref[:, 2 * d:3 * d])
    g_ref[...] = _silu(_dot(h, win_ref[:, 3 * d:4 * d]))

    row = lax.broadcasted_iota(jnp.int32, (cc, cc), 0)
    col = lax.broadcasted_iota(jnp.int32, (cc, cc), 1)
    tril = jnp.where(row >= col, 1.0, 0.0).astype(_BF16)
    for c in range(tb // cc):
        lf = logf[c * cc:(c + 1) * cc, :]
        p0 = lf.astype(_BF16)
        r1 = lf - p0.astype(_F32)
        p1 = r1.astype(_BF16)
        p2 = (r1 - p1.astype(_F32)).astype(_BF16)
        b_ref[c * cc:(c + 1) * cc, :] = _dot(tril, p0) + _dot(tril, p1) + _dot(tril, p2)

    levels = []
    s = cc // 2
    while s >= HG_DIAG:
        same = (row // (2 * s)) == (col // (2 * s))
        levels.append((s, same & ((row % (2 * s)) >= s) & ((col % (2 * s)) < s)))
        s //= 2
    diag_mask = ((row // HG_DIAG) == (col // HG_DIAG)) & (col <= row)

    nt = (((1,), (1,)), ((), ()))
    tn = (((0,), (0,)), ((), ()))
    for c in range(tb // cc):
        rows = slice(c * cc, (c + 1) * cc)
        q = q_ref[rows, :]
        k = k_ref[rows, :]
        b = b_ref[rows, :]
        v = v_ref[rows, :].astype(_BF16)
        b_last = b[cc - 1:cc, :]
        qe = (q * jnp.exp(b)).astype(_BF16)
        ke = (k * jnp.exp(b_last - b)).astype(_BF16)
        dec = jnp.exp(b_last)
        facs = []
        for s, _ in levels:
            ref = _block_row_bcast(b, 2 * s, s)
            facs.append(((q * jnp.exp(jnp.minimum(b - ref, 0.0))).astype(_BF16),
                         (k * jnp.exp(jnp.minimum(ref - b, 0.0))).astype(_BF16)))
        ref = _block_row_bcast(b, HG_DIAG, HG_DIAG // 2)
        qd = (q * jnp.exp(b - ref)).astype(_BF16)
        kd = (k * jnp.exp(ref - b)).astype(_BF16)
        for hd in range(N_HEADS):
            sl = slice(hd * HEAD_DIM, (hd + 1) * HEAD_DIM)
            att = jnp.where(diag_mask, lax.dot_general(qd[:, sl], kd[:, sl], nt,
                                                       preferred_element_type=_F32), 0.0)
            for (s, mask), (qf, kf) in zip(levels, facs):
                att = jnp.where(mask, lax.dot_general(qf[:, sl], kf[:, sl], nt,
                                                      preferred_element_type=_F32), att)
            st = st_ref[hd]
            o = _dot(att.astype(_BF16), v[:, sl]) + lax.dot_general(
                qe[:, sl], st.astype(_BF16), nt, preferred_element_type=_F32)
            st_ref[hd] = st * dec[:, sl] + lax.dot_general(
                v[:, sl], ke[:, sl], tn, preferred_element_type=_F32)
            oh_ref[rows, sl] = o

    gn = gn_ref[...]
    for hd in range(N_HEADS):
        sl = slice(hd * HEAD_DIM, (hd + 1) * HEAD_DIM)
        oh_ref[:, sl] = _rms_norm(oh_ref[:, sl], gn, GNORM_EPS)
    z = (oh_ref[...] * g_ref[...]).astype(_BF16)
    o_ref[...] = x_ref[...] + m[2:3] * _dot(z, wout_ref[...])


def _hg_call(x, mod_l, gain, w_in, lbp, gnorm, w_out, *, layer, tb=256):
    bsz, t, d = x.shape
    const2 = lambda b, i: (0, 0)
    return pl.pallas_call(
        functools.partial(_hg_kernel, layer=layer),
        grid=(bsz, t // tb),
        in_specs=[
            pl.BlockSpec((None, tb, d), lambda b, i: (b, i, 0)),
            pl.BlockSpec((None, 6, d), lambda b, i: (b, 0, 0)),
            pl.BlockSpec((1, d), const2),
            pl.BlockSpec(w_in.shape, const2, pipeline_mode=pl.Buffered(1)),
            pl.BlockSpec(lbp.shape, const2),
            pl.BlockSpec((1, HEAD_DIM), const2),
            pl.BlockSpec(w_out.shape, const2, pipeline_mode=pl.Buffered(1)),
        ],
        out_specs=pl.BlockSpec((None, tb, d), lambda b, i: (b, i, 0)),
        out_shape=jax.ShapeDtypeStruct((bsz, t, d), _F32),
        scratch_shapes=[pltpu.VMEM((tb, d), _F32) for _ in range(6)]
        + [pltpu.VMEM((N_HEADS, HEAD_DIM, HEAD_DIM), _F32)],
        compiler_params=pltpu.CompilerParams(
            dimension_semantics=("arbitrary", "arbitrary"),
            vmem_limit_bytes=VMEM_LIMIT),
        name="hgrn2",
    )(x, mod_l, gain, w_in, lbp, gnorm, w_out)


def kernel(x, c, mod_w, mod_b, norm_mix, norm_mlp, norm_final, rg_w_in, rg_conv_w, rg_conv_b,
           rg_w_a, rg_b_a, rg_w_x, rg_b_x, rg_lambda, rg_w_out, hg_w_in, hg_lower_bounds,
           hg_gnorm, hg_w_out, mlp_w1, mlp_w2):
    depth = mod_w.shape[0]
    bsz, t, d = x.shape
    mod = _mod_call(c, mod_w, mod_b).reshape(depth, bsz, 6, d)
    row = lambda p: p.reshape(1, -1)
    for layer in range(depth):
        j = layer // 2
        if layer % 2 == 0:
            wg = jnp.concatenate([rg_w_a[j], rg_w_x[j]], axis=-1).astype(_BF16)
            x = _rg_call(x, mod[layer], row(norm_mix[layer]), rg_w_in[j].astype(_BF16),
                         rg_conv_w[j], row(rg_conv_b[j]), wg, row(rg_b_a[j]), row(rg_b_x[j]),
                         row(rg_lambda[j]), rg_w_out[j].astype(_BF16))
        else:
            x = _hg_call(x, mod[layer], row(norm_mix[layer]), hg_w_in[j].astype(_BF16),
                         hg_lower_bounds, row(hg_gnorm[j]), hg_w_out[j].astype(_BF16),
                         layer=layer)
        x = _mlp_call(x, mod[layer], row(norm_mlp[layer]), mlp_w1[layer].astype(_BF16),
                      mlp_w2[layer].astype(_BF16), row(norm_final),
                      final=(layer == depth - 1))
    return x
```

```python
import functools

import jax
import jax.numpy as jnp
from jax import lax
from jax.experimental import pallas as pl
from jax.experimental.pallas import tpu as pltpu

NORM_EPS = 1e-6
GNORM_EPS = 1e-5
LRU_C = 8.0
LOG2_E = 1.4426950408889634
N_HEADS = 8
HEAD_DIM = 128
SUBLANES = 8
HG_CHUNK = 128
HG_DIAG = 16
VMEM_LIMIT = 56 * 1024 * 1024

_BF16 = jnp.bfloat16
_F32 = jnp.float32


def _dot(a, b):
    return jnp.dot(a, b, preferred_element_type=_F32)


def _sigmoid(x):
    return 0.5 * jnp.tanh(0.5 * x) + 0.5


def _silu(x):
    return x * _sigmoid(x)


def _gelu_tanh(x):
    return 0.5 * x * (1.0 + jnp.tanh(0.7978845608028654 * (x + 0.044715 * (x * x * x))))


def _rms_norm(x, gain, eps):
    ms = jnp.mean(x * x, axis=-1, keepdims=True)
    return x * lax.rsqrt(ms + eps) * gain


def _norm_mod(x, gain, scale, shift):
    return _rms_norm(x, gain, NORM_EPS) * (1.0 + scale) + shift


def _mod_kernel(c_ref, w_ref, b_ref, o_ref):
    cs = _silu(c_ref[...]).astype(_BF16)
    o_ref[...] = _dot(cs, w_ref[...].astype(_BF16)) + b_ref[...]


def _mod_call(c, mod_w, mod_b):
    depth, d, n6 = mod_w.shape
    bsz = c.shape[0]
    tn = 2048
    return pl.pallas_call(
        _mod_kernel,
        grid=(depth, n6 // tn),
        in_specs=[
            pl.BlockSpec((bsz, d), lambda l, j: (0, 0)),
            pl.BlockSpec((None, d, tn), lambda l, j: (l, 0, j)),
            pl.BlockSpec((None, 1, tn), lambda l, j: (l, 0, j)),
        ],
        out_specs=pl.BlockSpec((None, bsz, tn), lambda l, j: (l, 0, j)),
        out_shape=jax.ShapeDtypeStruct((depth, bsz, n6), _F32),
        compiler_params=pltpu.CompilerParams(
            dimension_semantics=("arbitrary", "arbitrary"),
            vmem_limit_bytes=VMEM_LIMIT),
        name="adaln_mod",
    )(c, mod_w, mod_b.reshape(depth, 1, n6))


def _mlp_kernel(x_ref, mod_ref, gain_ref, w1_ref, w2_ref, gf_ref, o_ref, hid_ref, *, final, fc):
    x = x_ref[...]
    m = mod_ref[...]
    h = _norm_mod(x, gain_ref[...], m[4:5], m[3:4]).astype(_BF16)
    d_ff = w1_ref.shape[1]
    for c in range(d_ff // fc):
        t = jnp.maximum(_dot(h, w1_ref[:, c * fc:(c + 1) * fc]), 0.0)
        hid_ref[:, c * fc:(c + 1) * fc] = (t * t).astype(_BF16)
    y = _dot(hid_ref[...], w2_ref[...])
    xn = x + m[5:6] * y
    if final:
        xn = _rms_norm(xn, gf_ref[...], NORM_EPS)
    o_ref[...] = xn


def _mlp_call(x, mod_l, gain, w1, w2, gain_final, *, final, tm=1024, fc=1024):
    bsz, t, d = x.shape
    d_ff = w1.shape[1]
    nt = t // tm
    const = lambda b, i: (0, 0)
    return pl.pallas_call(
        functools.partial(_mlp_kernel, final=final, fc=fc),
        grid=(bsz, nt),
        in_specs=[
            pl.BlockSpec((None, tm, d), lambda b, i: (b, i, 0)),
            pl.BlockSpec((None, 6, d), lambda b, i: (b, 0, 0)),
            pl.BlockSpec((1, d), const),
            pl.BlockSpec((d, d_ff), const, pipeline_mode=pl.Buffered(1)),
            pl.BlockSpec((d_ff, d), const, pipeline_mode=pl.Buffered(1)),
            pl.BlockSpec((1, d), const),
        ],
        out_specs=pl.BlockSpec((None, tm, d), lambda b, i: (b, i, 0)),
        out_shape=jax.ShapeDtypeStruct((bsz, t, d), _F32),
        scratch_shapes=[pltpu.VMEM((tm, d_ff), _BF16)],
        compiler_params=pltpu.CompilerParams(
            dimension_semantics=("arbitrary", "arbitrary"),
            vmem_limit_bytes=VMEM_LIMIT),
        name="mlp_final" if final else "mlp",
    )(x, mod_l, gain, w1, w2, gain_final)


def _rg_kernel(x_ref, mod_ref, gain_ref, win_ref, cw_ref, cb_ref, wg_ref, ba_ref, bx_ref,
               lam_ref, wout_ref, o_ref, ext_ref, a_ref, u_ref, gate_ref, hc_ref):
    tb, d = x_ref.shape
    tblk = pl.program_id(1)

    @pl.when(tblk == 0)
    def _():
        ext_ref[0:SUBLANES, :] = jnp.zeros((SUBLANES, d), _F32)
        hc_ref[...] = jnp.zeros_like(hc_ref)

    m = mod_ref[...]
    h = _norm_mod(x_ref[...], gain_ref[...], m[1:2], m[0:1]).astype(_BF16)
    gate_ref[...] = _gelu_tanh(_dot(h, win_ref[:, d:]))
    ext_ref[SUBLANES:, :] = _dot(h, win_ref[:, :d])

    cw = cw_ref[...]
    n_tap = cw.shape[0]
    xc = cb_ref[...]
    for tap in range(n_tap):
        off = SUBLANES - (n_tap - 1) + tap
        xc = xc + ext_ref[off:off + tb, :] * cw[tap:tap + 1, :]
    ext_ref[0:SUBLANES, :] = ext_ref[tb:tb + SUBLANES, :]

    lam = lam_ref[...]
    nl = -lam
    softplus = jnp.maximum(nl, 0.0) + jnp.log1p(jnp.exp(-jnp.abs(nl)))
    rate = (-LRU_C * LOG2_E) * softplus
    first_row = jnp.logical_and(
        lax.broadcasted_iota(jnp.int32, (tb, HEAD_DIM), 0) == 0, tblk == 0)
    for hd in range(N_HEADS):
        sl = slice(hd * HEAD_DIM, (hd + 1) * HEAD_DIM)
        xh = xc[:, sl]
        gp = _dot(xh.astype(_BF16), wg_ref[hd])
        r = _sigmoid(gp[:, :HEAD_DIM] + ba_ref[:, sl])
        i = _sigmoid(gp[:, HEAD_DIM:] + bx_ref[:, sl])
        a = jnp.exp2(r * rate[:, sl])
        mult = jnp.where(first_row, 1.0, jnp.sqrt(1.0 - a * a))
        a_ref[:, sl] = a
        u_ref[:, sl] = mult * (i * xh)

    sub = lax.broadcasted_iota(jnp.int32, (SUBLANES, d), 0)

    def tile_step(j, carry):
        r0 = pl.multiple_of(j * SUBLANES, SUBLANES)
        a = a_ref[pl.ds(r0, SUBLANES), :]
        u = u_ref[pl.ds(r0, SUBLANES), :]
        for shift in (1, 2, 4):
            keep = sub >= shift
            u = u + a * jnp.where(keep, pltpu.roll(u, shift, 0), 0.0)
            a = a * jnp.where(keep, pltpu.roll(a, shift, 0), 1.0)
        hh = u + a * carry
        u_ref[pl.ds(r0, SUBLANES), :] = hh
        return jnp.broadcast_to(hh[SUBLANES - 1:SUBLANES, :], (SUBLANES, d))

    hc_ref[...] = lax.fori_loop(0, tb // SUBLANES, tile_step, hc_ref[...], unroll=2)

    z = (u_ref[...] * gate_ref[...]).astype(_BF16)
    o_ref[...] = x_ref[...] + m[2:3] * _dot(z, wout_ref[...])


def _rg_call(x, mod_l, gain, w_in, conv_w, conv_b, wg, b_a, b_x, lam, w_out, *, tb=512):
    bsz, t, d = x.shape
    const2 = lambda b, i: (0, 0)
    const3 = lambda b, i: (0, 0, 0)
    return pl.pallas_call(
        _rg_kernel,
        grid=(bsz, t // tb),
        in_specs=[
            pl.BlockSpec((None, tb, d), lambda b, i: (b, i, 0)),
            pl.BlockSpec((None, 6, d), lambda b, i: (b, 0, 0)),
            pl.BlockSpec((1, d), const2),
            pl.BlockSpec(w_in.shape, const2, pipeline_mode=pl.Buffered(1)),
            pl.BlockSpec(conv_w.shape, const2),
            pl.BlockSpec((1, d), const2),
            pl.BlockSpec(wg.shape, const3, pipeline_mode=pl.Buffered(1)),
            pl.BlockSpec((1, d), const2),
            pl.BlockSpec((1, d), const2),
            pl.BlockSpec((1, d), const2),
            pl.BlockSpec(w_out.shape, const2, pipeline_mode=pl.Buffered(1)),
        ],
        out_specs=pl.BlockSpec((None, tb, d), lambda b, i: (b, i, 0)),
        out_shape=jax.ShapeDtypeStruct((bsz, t, d), _F32),
        scratch_shapes=[
            pltpu.VMEM((tb + SUBLANES, d), _F32),
            pltpu.VMEM((tb, d), _F32),
            pltpu.VMEM((tb, d), _F32),
            pltpu.VMEM((tb, d), _F32),
            pltpu.VMEM((SUBLANES, d), _F32),
        ],
        compiler_params=pltpu.CompilerParams(
            dimension_semantics=("arbitrary", "arbitrary"),
            vmem_limit_bytes=VMEM_LIMIT),
        name="rglru",
    )(x, mod_l, gain, w_in, conv_w, conv_b, wg, b_a, b_x, lam, w_out)


def _block_row_bcast(b, block, row):
    n, d = b.shape
    parts = []
    for s in range(0, n, block):
        parts.append(jnp.broadcast_to(b[s + row:s + row + 1, :], (block, d)))
    return jnp.concatenate(parts, axis=0)


def _hg_kernel(x_ref, mod_ref, gain_ref, win_ref, lbp_ref, gn_ref, wout_ref, o_ref,
               q_ref, k_ref, b_ref, v_ref, g_ref, oh_ref, st_ref, *, layer):
    tb, d = x_ref.shape
    cc = HG_CHUNK
    tblk = pl.program_id(1)

    @pl.when(tblk == 0)
    def _():
        st_ref[...] = jnp.zeros_like(st_ref)

    lbp = lbp_ref[...]
    depth = lbp.shape[0]
    mx = lbp[0:1]
    for j in range(1, depth):
        mx = jnp.maximum(mx, lbp[j:j + 1])
    es = [jnp.exp(lbp[j:j + 1] - mx) for j in range(depth)]
    tot = es[0]
    for j in range(1, depth):
        tot = tot + es[j]
    num = es[1]
    for j in range(2, layer + 1):
        num = num + es[j]
    lb = num / tot

    m = mod_ref[...]
    h = _norm_mod(x_ref[...], gain_ref[...], m[1:2], m[0:1]).astype(_BF16)
    q_ref[...] = _silu(_dot(h, win_ref[:, 0:d]))
    fg = lb + (1.0 - lb) * _sigmoid(_dot(h, win_ref[:, d:2 * d]))
    k_ref[...] = 1.0 - fg
    logf = jnp.log2(fg)
    v_ref[...] = _dot(h, win_ref[:, 2 * d:3 * d])
    g_ref[...] = _silu(_dot(h, win_ref[:, 3 * d:4 * d]))

    row = lax.broadcasted_iota(jnp.int32, (cc, cc), 0)
    col = lax.broadcasted_iota(jnp.int32, (cc, cc), 1)
    tril = jnp.where(row >= col, 1.0, 0.0).astype(_BF16)
    tril2 = jnp.concatenate([tril, tril], axis=1)
    for c in range(tb // cc):
        lf = logf[c * cc:(c + 1) * cc, :]
        p0 = lf.astype(_BF16)
        p1 = (lf - p0.astype(_F32)).astype(_BF16)
        b_ref[c * cc:(c + 1) * cc, :] = _dot(tril2, jnp.concatenate([p0, p1], axis=0))

    levels = []
    s = cc // 2
    while s >= HG_DIAG:
        same = (row // (2 * s)) == (col // (2 * s))
        levels.append((s, same & ((row % (2 * s)) >= s) & ((col % (2 * s)) < s)))
        s //= 2
    diag_mask = ((row // HG_DIAG) == (col // HG_DIAG)) & (col <= row)

    nt = (((1,), (1,)), ((), ()))
    tn = (((0,), (0,)), ((), ()))
    for c in range(tb // cc):
        rows = slice(c * cc, (c + 1) * cc)
        q = q_ref[rows, :]
        k = k_ref[rows, :]
        b = b_ref[rows, :]
        v = v_ref[rows, :].astype(_BF16)
        b_last = b[cc - 1:cc, :]
        qe = (q * jnp.exp2(b)).astype(_BF16)
        ke = (k * jnp.exp2(b_last - b)).astype(_BF16)
        dec = jnp.exp2(b_last)
        facs = []
        for s, _ in levels:
            ref = _block_row_bcast(b, 2 * s, s)
            facs.append(((q * jnp.exp2(b - ref)).astype(_BF16),
                         (k * jnp.exp2(ref - b)).astype(_BF16)))
        ref = _block_row_bcast(b, HG_DIAG, HG_DIAG // 2)
        qd = (q * jnp.exp2(b - ref)).astype(_BF16)
        kd = (k * jnp.exp2(ref - b)).astype(_BF16)
        for hd in range(N_HEADS):
            sl = slice(hd * HEAD_DIM, (hd + 1) * HEAD_DIM)
            att = jnp.where(diag_mask, lax.dot_general(qd[:, sl], kd[:, sl], nt,
                                                       preferred_element_type=_F32), 0.0)
            for (s, mask), (qf, kf) in zip(levels, facs):
                att = jnp.where(mask, lax.dot_general(qf[:, sl], kf[:, sl], nt,
                                                      preferred_element_type=_F32), att)
            st = st_ref[hd]
            o = _dot(att.astype(_BF16), v[:, sl]) + lax.dot_general(
                qe[:, sl], st.astype(_BF16), nt, preferred_element_type=_F32)
            st_ref[hd] = st * dec[:, sl] + lax.dot_general(
                v[:, sl], ke[:, sl], tn, preferred_element_type=_F32)
            oh_ref[rows, sl] = o

    gn = gn_ref[...]
    for hd in range(N_HEADS):
        sl = slice(hd * HEAD_DIM, (hd + 1) * HEAD_DIM)
        oh_ref[:, sl] = _rms_norm(oh_ref[:, sl], gn, GNORM_EPS)
    z = (oh_ref[...] * g_ref[...]).astype(_BF16)
    o_ref[...] = x_ref[...] + m[2:3] * _dot(z, wout_ref[...])


def _hg_call(x, mod_l, gain, w_in, lbp, gnorm, w_out, *, layer, tb=512):
    bsz, t, d = x.shape
    const2 = lambda b, i: (0, 0)
    return pl.pallas_call(
        functools.partial(_hg_kernel, layer=layer),
        grid=(bsz, t // tb),
        in_specs=[
            pl.BlockSpec((None, tb, d), lambda b, i: (b, i, 0)),
            pl.BlockSpec((None, 6, d), lambda b, i: (b, 0, 0)),
            pl.BlockSpec((1, d), const2),
            pl.BlockSpec(w_in.shape, const2, pipeline_mode=pl.Buffered(1)),
            pl.BlockSpec(lbp.shape, const2),
            pl.BlockSpec((1, HEAD_DIM), const2),
            pl.BlockSpec(w_out.shape, const2, pipeline_mode=pl.Buffered(1)),
        ],
        out_specs=pl.BlockSpec((None, tb, d), lambda b, i: (b, i, 0)),
        out_shape=jax.ShapeDtypeStruct((bsz, t, d), _F32),
        scratch_shapes=[pltpu.VMEM((tb, d), _F32) for _ in range(6)]
        + [pltpu.VMEM((N_HEADS, HEAD_DIM, HEAD_DIM), _F32)],
        compiler_params=pltpu.CompilerParams(
            dimension_semantics=("arbitrary", "arbitrary"),
            vmem_limit_bytes=VMEM_LIMIT),
        name="hgrn2",
    )(x, mod_l, gain, w_in, lbp, gnorm, w_out)


def kernel(x, c, mod_w, mod_b, norm_mix, norm_mlp, norm_final, rg_w_in, rg_conv_w, rg_conv_b,
           rg_w_a, rg_b_a, rg_w_x, rg_b_x, rg_lambda, rg_w_out, hg_w_in, hg_lower_bounds,
           hg_gnorm, hg_w_out, mlp_w1, mlp_w2):
    depth = mod_w.shape[0]
    bsz, t, d = x.shape
    mod = _mod_call(c, mod_w, mod_b).reshape(depth, bsz, 6, d)
    row = lambda p: p.reshape(1, -1)
    for layer in range(depth):
        j = layer // 2
        if layer % 2 == 0:
            wg = jnp.concatenate([rg_w_a[j], rg_w_x[j]], axis=-1).astype(_BF16)
            x = _rg_call(x, mod[layer], row(norm_mix[layer]), rg_w_in[j].astype(_BF16),
                         rg_conv_w[j], row(rg_conv_b[j]), wg, row(rg_b_a[j]), row(rg_b_x[j]),
                         row(rg_lambda[j]), rg_w_out[j].astype(_BF16))
        else:
            x = _hg_call(x, mod[layer], row(norm_mix[layer]), hg_w_in[j].astype(_BF16),
                         hg_lower_bounds, row(hg_gnorm[j]), hg_w_out[j].astype(_BF16),
                         layer=layer)
        x = _mlp_call(x, mod[layer], row(norm_mlp[layer]), mlp_w1[layer].astype(_BF16),
                      mlp_w2[layer].astype(_BF16), row(norm_final),
                      final=(layer == depth - 1))
    return x
```

```python
import functools

import jax
import jax.numpy as jnp
from jax import lax
from jax.experimental import pallas as pl
from jax.experimental.pallas import tpu as pltpu

NORM_EPS = 1e-6
GNORM_EPS = 1e-5
LRU_C = 8.0
LOG2_E = 1.4426950408889634
N_HEADS = 8
HEAD_DIM = 128
SUBLANES = 8
HG_CHUNK = 128
HG_DIAG = 32
VMEM_LIMIT = 56 * 1024 * 1024

_BF16 = jnp.bfloat16
_F32 = jnp.float32
_NT = (((1,), (1,)), ((), ()))
_TN = (((0,), (0,)), ((), ()))


def _dot(a, b):
    return jnp.dot(a, b, preferred_element_type=_F32)


def _sigmoid(x):
    return 0.5 * jnp.tanh(0.5 * x) + 0.5


def _silu(x):
    return x * _sigmoid(x)


def _gelu_tanh(x):
    return 0.5 * x * (1.0 + jnp.tanh(0.7978845608028654 * (x + 0.044715 * (x * x * x))))


def _rms_norm(x, gain, eps):
    ms = jnp.mean(x * x, axis=-1, keepdims=True)
    return x * lax.rsqrt(ms + eps) * gain


def _norm_mod(x, gain, scale, shift):
    return _rms_norm(x, gain, NORM_EPS) * (1.0 + scale) + shift


def _mod_kernel(c_ref, w_ref, b_ref, o_ref):
    cs = _silu(c_ref[...]).astype(_BF16)
    o_ref[...] = _dot(cs, w_ref[...].astype(_BF16)) + b_ref[...]


def _mod_call(c, mod_w, mod_b):
    depth, d, n6 = mod_w.shape
    bsz = c.shape[0]
    tn = 2048
    return pl.pallas_call(
        _mod_kernel,
        grid=(depth, n6 // tn),
        in_specs=[
            pl.BlockSpec((bsz, d), lambda l, j: (0, 0)),
            pl.BlockSpec((None, d, tn), lambda l, j: (l, 0, j)),
            pl.BlockSpec((None, 1, tn), lambda l, j: (l, 0, j)),
        ],
        out_specs=pl.BlockSpec((None, bsz, tn), lambda l, j: (l, 0, j)),
        out_shape=jax.ShapeDtypeStruct((depth, bsz, n6), _F32),
        compiler_params=pltpu.CompilerParams(
            dimension_semantics=("arbitrary", "arbitrary"),
            vmem_limit_bytes=VMEM_LIMIT),
        name="adaln_mod",
    )(c, mod_w, mod_b.reshape(depth, 1, n6))


def _mlp_kernel(x_ref, mod_ref, gain_ref, w1_ref, w2_ref, gf_ref, o_ref, hid_ref, *, final, fc):
    x = x_ref[...]
    m = mod_ref[...]
    h = _norm_mod(x, gain_ref[...], m[4:5], m[3:4]).astype(_BF16)
    d_ff = w1_ref.shape[1]
    for c in range(d_ff // fc):
        t = jnp.maximum(_dot(h, w1_ref[:, c * fc:(c + 1) * fc]), 0.0)
        hid_ref[:, c * fc:(c + 1) * fc] = (t * t).astype(_BF16)
    y = _dot(hid_ref[...], w2_ref[...])
    xn = x + m[5:6] * y
    if final:
        xn = _rms_norm(xn, gf_ref[...], NORM_EPS)
    o_ref[...] = xn


def _mlp_call(x, mod_l, gain, w1, w2, gain_final, *, final, tm=1024, fc=1024):
    bsz, t, d = x.shape
    d_ff = w1.shape[1]
    nt = t // tm
    const = lambda b, i: (0, 0)
    return pl.pallas_call(
        functools.partial(_mlp_kernel, final=final, fc=fc),
        grid=(bsz, nt),
        in_specs=[
            pl.BlockSpec((None, tm, d), lambda b, i: (b, i, 0)),
            pl.BlockSpec((None, 6, d), lambda b, i: (b, 0, 0)),
            pl.BlockSpec((1, d), const),
            pl.BlockSpec((d, d_ff), const, pipeline_mode=pl.Buffered(1)),
            pl.BlockSpec((d_ff, d), const, pipeline_mode=pl.Buffered(1)),
            pl.BlockSpec((1, d), const),
        ],
        out_specs=pl.BlockSpec((None, tm, d), lambda b, i: (b, i, 0)),
        out_shape=jax.ShapeDtypeStruct((bsz, t, d), _F32),
        scratch_shapes=[pltpu.VMEM((tm, d_ff), _BF16)],
        compiler_params=pltpu.CompilerParams(
            dimension_semantics=("arbitrary", "arbitrary"),
            vmem_limit_bytes=VMEM_LIMIT),
        name="mlp_final" if final else "mlp",
    )(x, mod_l, gain, w1, w2, gain_final)


def _rg_kernel(x_ref, mod_ref, gain_ref, win_ref, cw_ref, cb_ref, wg_ref, ba_ref, bx_ref,
               lam_ref, wout_ref, o_ref, ext_ref, a_ref, u_ref, gate_ref, hc_ref):
    tb, d = x_ref.shape
    tblk = pl.program_id(1)

    @pl.when(tblk == 0)
    def _():
        ext_ref[0:SUBLANES, :] = jnp.zeros((SUBLANES, d), _F32)
        hc_ref[...] = jnp.zeros_like(hc_ref)

    m = mod_ref[...]
    h = _norm_mod(x_ref[...], gain_ref[...], m[1:2], m[0:1]).astype(_BF16)
    gate_ref[...] = _gelu_tanh(_dot(h, win_ref[:, d:]))
    ext_ref[SUBLANES:, :] = _dot(h, win_ref[:, :d])

    cw = cw_ref[...]
    n_tap = cw.shape[0]
    xc = cb_ref[...]
    for tap in range(n_tap):
        off = SUBLANES - (n_tap - 1) + tap
        xc = xc + ext_ref[off:off + tb, :] * cw[tap:tap + 1, :]
    ext_ref[0:SUBLANES, :] = ext_ref[tb:tb + SUBLANES, :]

    lam = lam_ref[...]
    nl = -lam
    softplus = jnp.maximum(nl, 0.0) + jnp.log1p(jnp.exp(-jnp.abs(nl)))
    rate = (-LRU_C * LOG2_E) * softplus
    first_row = jnp.logical_and(
        lax.broadcasted_iota(jnp.int32, (tb, HEAD_DIM), 0) == 0, tblk == 0)
    for hd in range(N_HEADS):
        sl = slice(hd * HEAD_DIM, (hd + 1) * HEAD_DIM)
        xh = xc[:, sl]
        gp = _dot(xh.astype(_BF16), wg_ref[hd])
        r = _sigmoid(gp[:, :HEAD_DIM] + ba_ref[:, sl])
        i = _sigmoid(gp[:, HEAD_DIM:] + bx_ref[:, sl])
        a = jnp.exp2(r * rate[:, sl])
        mult = jnp.where(first_row, 1.0, jnp.sqrt(1.0 - a * a))
        a_ref[:, sl] = a
        u_ref[:, sl] = mult * (i * xh)

    sub = lax.broadcasted_iota(jnp.int32, (SUBLANES, d), 0)

    def tile_step(j, carry):
        r0 = pl.multiple_of(j * SUBLANES, SUBLANES)
        a = a_ref[pl.ds(r0, SUBLANES), :]
        u = u_ref[pl.ds(r0, SUBLANES), :]
        for shift in (1, 2, 4):
            keep = sub >= shift
            u = u + a * jnp.where(keep, pltpu.roll(u, shift, 0), 0.0)
            a = a * jnp.where(keep, pltpu.roll(a, shift, 0), 1.0)
        hh = u + a * carry
        u_ref[pl.ds(r0, SUBLANES), :] = hh
        return jnp.broadcast_to(hh[SUBLANES - 1:SUBLANES, :], (SUBLANES, d))

    hc_ref[...] = lax.fori_loop(0, tb // SUBLANES, tile_step, hc_ref[...], unroll=2)

    z = (u_ref[...] * gate_ref[...]).astype(_BF16)
    o_ref[...] = x_ref[...] + m[2:3] * _dot(z, wout_ref[...])


def _rg_call(x, mod_l, gain, w_in, conv_w, conv_b, wg, b_a, b_x, lam, w_out, *, tb=512):
    bsz, t, d = x.shape
    const2 = lambda b, i: (0, 0)
    const3 = lambda b, i: (0, 0, 0)
    return pl.pallas_call(
        _rg_kernel,
        grid=(bsz, t // tb),
        in_specs=[
            pl.BlockSpec((None, tb, d), lambda b, i: (b, i, 0)),
            pl.BlockSpec((None, 6, d), lambda b, i: (b, 0, 0)),
            pl.BlockSpec((1, d), const2),
            pl.BlockSpec(w_in.shape, const2, pipeline_mode=pl.Buffered(1)),
            pl.BlockSpec(conv_w.shape, const2),
            pl.BlockSpec((1, d), const2),
            pl.BlockSpec(wg.shape, const3, pipeline_mode=pl.Buffered(1)),
            pl.BlockSpec((1, d), const2),
            pl.BlockSpec((1, d), const2),
            pl.BlockSpec((1, d), const2),
            pl.BlockSpec(w_out.shape, const2, pipeline_mode=pl.Buffered(1)),
        ],
        out_specs=pl.BlockSpec((None, tb, d), lambda b, i: (b, i, 0)),
        out_shape=jax.ShapeDtypeStruct((bsz, t, d), _F32),
        scratch_shapes=[
            pltpu.VMEM((tb + SUBLANES, d), _F32),
            pltpu.VMEM((tb, d), _F32),
            pltpu.VMEM((tb, d), _F32),
            pltpu.VMEM((tb, d), _F32),
            pltpu.VMEM((SUBLANES, d), _F32),
        ],
        compiler_params=pltpu.CompilerParams(
            dimension_semantics=("arbitrary", "arbitrary"),
            vmem_limit_bytes=VMEM_LIMIT),
        name="rglru",
    )(x, mod_l, gain, w_in, conv_w, conv_b, wg, b_a, b_x, lam, w_out)


def _block_row_bcast(b, block, row):
    n, d = b.shape
    parts = []
    for s in range(0, n, block):
        parts.append(jnp.broadcast_to(b[s + row:s + row + 1, :], (block, d)))
    return jnp.concatenate(parts, axis=0)


def _hg_lower_bound(lbp_ref, layer):
    lbp = lbp_ref[...]
    depth = lbp.shape[0]
    mx = lbp[0:1]
    for j in range(1, depth):
        mx = jnp.maximum(mx, lbp[j:j + 1])
    es = [jnp.exp(lbp[j:j + 1] - mx) for j in range(depth)]
    tot = es[0]
    for j in range(1, depth):
        tot = tot + es[j]
    num = es[1]
    for j in range(2, layer + 1):
        num = num + es[j]
    return num / tot


class _HgProducer:
    def __init__(self, x_ref, mod_ref, gain_ref, win_ref, lbp_ref, slot, h_ref, layer):
        self.x_ref, self.win_ref, self.h_ref = x_ref, win_ref, h_ref
        self.q_ref, self.k_ref, self.b_ref, self.v_ref, self.g_ref = slot
        self.tb, self.d = x_ref.shape
        self.lb = _hg_lower_bound(lbp_ref, layer)
        m = mod_ref[...]
        self.scale = gain_ref[...] * (1.0 + m[1:2])
        self.shift = m[0:1]
        cc = HG_CHUNK
        row = lax.broadcasted_iota(jnp.int32, (cc, cc), 0)
        col = lax.broadcasted_iota(jnp.int32, (cc, cc), 1)
        tril = jnp.where(row >= col, 1.0, 0.0).astype(_BF16)
        self.tril2 = jnp.concatenate([tril, tril], axis=1)

    def norm(self):
        x = self.x_ref[...]
        ms = jnp.mean(x * x, axis=-1, keepdims=True)
        self.h_ref[...] = (x * lax.rsqrt(ms + NORM_EPS) * self.scale + self.shift).astype(_BF16)

    def _proj(self, part, rows, cols):
        w = self.win_ref[:, part * self.d + cols.start:part * self.d + cols.stop]
        return _dot(self.h_ref[rows, :], w)

    def query(self, rows, cols):
        self.q_ref[rows, cols] = _silu(self._proj(0, rows, cols))

    def forget(self, rows, cols):
        lb = self.lb[:, cols]
        fg = lb + (1.0 - lb) * _sigmoid(self._proj(1, rows, cols))
        self.k_ref[rows, cols] = 1.0 - fg
        lf = jnp.log2(fg)
        p0 = lf.astype(_BF16)
        p1 = (lf - p0.astype(_F32)).astype(_BF16)
        for r in range(0, rows.stop - rows.start, HG_CHUNK):
            self.b_ref[rows.start + r:rows.start + r + HG_CHUNK, cols] = _dot(
                self.tril2, jnp.concatenate([p0[r:r + HG_CHUNK], p1[r:r + HG_CHUNK]], axis=0))

    def value(self, rows, cols):
        self.v_ref[rows, cols] = self._proj(2, rows, cols).astype(_BF16)

    def gate(self, rows, cols):
        self.g_ref[rows, cols] = _silu(self._proj(3, rows, cols))


class _HgConsumer:
    def __init__(self, x_ref, mod_ref, gn_ref, wout_ref, slot, st_ref, z_ref, o_ref):
        self.x_ref, self.wout_ref, self.st_ref = x_ref, wout_ref, st_ref
        self.z_ref, self.o_ref = z_ref, o_ref
        self.q_ref, self.k_ref, self.b_ref, self.v_ref, self.g_ref = slot
        self.res_gate = mod_ref[2:3, :]
        self.gn = gn_ref[...]
        cc = HG_CHUNK
        row = lax.broadcasted_iota(jnp.int32, (cc, cc), 0)
        col = lax.broadcasted_iota(jnp.int32, (cc, cc), 1)
        self.levels = []
        s = cc // 2
        while s >= HG_DIAG:
            same = (row // (2 * s)) == (col // (2 * s))
            self.levels.append((s, same & ((row % (2 * s)) >= s) & ((col % (2 * s)) < s)))
            s //= 2
        self.diag_mask = ((row // HG_DIAG) == (col // HG_DIAG)) & (col <= row)
        self.outs = None

    def prep(self, rows):
        cc = HG_CHUNK
        q = self.q_ref[rows, :]
        k = self.k_ref[rows, :]
        b = self.b_ref[rows, :]
        b_last = b[cc - 1:cc, :]
        self.v = self.v_ref[rows, :]
        self.qe = (q * jnp.exp2(b)).astype(_BF16)
        self.ke = (k * jnp.exp2(b_last - b)).astype(_BF16)
        self.dec = jnp.exp2(b_last)
        self.facs = []
        for s, _ in self.levels:
            ref = _block_row_bcast(b, 2 * s, s)
            self.facs.append(((q * jnp.exp2(b - ref)).astype(_BF16),
                              (k * jnp.exp2(ref - b)).astype(_BF16)))
        ref = _block_row_bcast(b, HG_DIAG, HG_DIAG // 2)
        self.qd = (q * jnp.exp2(b - ref)).astype(_BF16)
        self.kd = (k * jnp.exp2(ref - b)).astype(_BF16)
        self.outs = []

    def scores(self, hd):
        sl = slice(hd * HEAD_DIM, (hd + 1) * HEAD_DIM)
        att = jnp.where(self.diag_mask, lax.dot_general(
            self.qd[:, sl], self.kd[:, sl], _NT, preferred_element_type=_F32), 0.0)
        for (s, mask), (qf, kf) in zip(self.levels, self.facs):
            att = jnp.where(mask, lax.dot_general(
                qf[:, sl], kf[:, sl], _NT, preferred_element_type=_F32), att)
        return att.astype(_BF16)

    def head(self, hd, att):
        sl = slice(hd * HEAD_DIM, (hd + 1) * HEAD_DIM)
        st = self.st_ref[hd]
        o = _dot(att, self.v[:, sl]) + lax.dot_general(
            self.qe[:, sl], st.astype(_BF16), _NT, preferred_element_type=_F32)
        self.st_ref[hd] = st * self.dec[:, sl] + lax.dot_general(
            self.v[:, sl], self.ke[:, sl], _TN, preferred_element_type=_F32)
        self.outs.append(_rms_norm(o, self.gn, GNORM_EPS))

    def chunk(self, rows):
        self.prep(rows)
        atts = [self.scores(hd) for hd in range(N_HEADS)]
        for hd in range(N_HEADS):
            self.head(hd, atts[hd])
        self.finish(rows)

    def finish(self, rows):
        self.z_ref[rows, :] = (
            jnp.concatenate(self.outs, axis=1) * self.g_ref[rows, :]).astype(_BF16)

    def out_proj(self, rows, cols):
        y = _dot(self.z_ref[rows, :], self.wout_ref[:, cols])
        self.o_ref[rows, cols] = self.x_ref[rows, cols] + self.res_gate[:, cols] * y


def _hg_kernel(xp_ref, modp_ref, xc_ref, modc_ref, gain_ref, win_ref, lbp_ref, gn_ref, wout_ref,
               o_ref, *scratch, layer, nt_blocks):
    slots = (scratch[0:5], scratch[5:10])
    st_ref, h_ref, z_ref = scratch[10:13]
    tb, d = xp_ref.shape
    s = pl.program_id(0)

    @pl.when(s == 0)
    def _():
        for r in slots[1]:
            r[...] = jnp.zeros_like(r)

    @pl.when(jnp.logical_or(s == 0, (s - 1) % nt_blocks == 0))
    def _():
        st_ref[...] = jnp.zeros_like(st_ref)

    def step(p_slot, c_slot):
        prod = _HgProducer(xp_ref, modp_ref, gain_ref, win_ref, lbp_ref, p_slot, h_ref, layer)
        cons = _HgConsumer(xc_ref, modc_ref, gn_ref, wout_ref, c_slot, st_ref, z_ref, o_ref)
        block, full = slice(0, tb), slice(0, d)
        prod.norm()
        for piece in (prod.query, prod.forget, prod.value, prod.gate):
            piece(block, full)
        for c in range(tb // HG_CHUNK):
            cons.chunk(slice(c * HG_CHUNK, (c + 1) * HG_CHUNK))
        cons.out_proj(block, full)

    @pl.when(s % 2 == 0)
    def _():
        step(slots[0], slots[1])

    @pl.when(s % 2 == 1)
    def _():
        step(slots[1], slots[0])


def _hg_call(x, mod_l, gain, w_in, lbp, gnorm, w_out, *, layer, tb=512):
    bsz, t, d = x.shape
    ntb = t // tb
    n_blocks = bsz * ntb
    const2 = lambda s: (0, 0)
    p_blk = lambda s: jnp.minimum(s, n_blocks - 1)
    c_blk = lambda s: jnp.maximum(s - 1, 0)
    x_p = lambda s: (p_blk(s) // ntb, p_blk(s) % ntb, 0)
    x_c = lambda s: (c_blk(s) // ntb, c_blk(s) % ntb, 0)
    slot = [pltpu.VMEM((tb, d), _F32), pltpu.VMEM((tb, d), _F32), pltpu.VMEM((tb, d), _F32),
            pltpu.VMEM((tb, d), _BF16), pltpu.VMEM((tb, d), _F32)]
    return pl.pallas_call(
        functools.partial(_hg_kernel, layer=layer, nt_blocks=ntb),
        grid=(n_blocks + 1,),
        in_specs=[
            pl.BlockSpec((None, tb, d), x_p),
            pl.BlockSpec((None, 6, d), lambda s: (p_blk(s) // ntb, 0, 0)),
            pl.BlockSpec((None, tb, d), x_c),
            pl.BlockSpec((None, 6, d), lambda s: (c_blk(s) // ntb, 0, 0)),
            pl.BlockSpec((1, d), const2),
            pl.BlockSpec(w_in.shape, const2, pipeline_mode=pl.Buffered(1)),
            pl.BlockSpec(lbp.shape, const2),
            pl.BlockSpec((1, HEAD_DIM), const2),
            pl.BlockSpec(w_out.shape, const2, pipeline_mode=pl.Buffered(1)),
        ],
        out_specs=pl.BlockSpec((None, tb, d), x_c),
        out_shape=jax.ShapeDtypeStruct((bsz, t, d), _F32),
        scratch_shapes=slot + slot + [
            pltpu.VMEM((N_HEADS, HEAD_DIM, HEAD_DIM), _F32),
            pltpu.VMEM((tb, d), _BF16),
            pltpu.VMEM((tb, d), _BF16)],
        compiler_params=pltpu.CompilerParams(
            dimension_semantics=("arbitrary",),
            vmem_limit_bytes=VMEM_LIMIT),
        name="hgrn2",
    )(x, mod_l, x, mod_l, gain, w_in, lbp, gnorm, w_out)


def kernel(x, c, mod_w, mod_b, norm_mix, norm_mlp, norm_final, rg_w_in, rg_conv_w, rg_conv_b,
           rg_w_a, rg_b_a, rg_w_x, rg_b_x, rg_lambda, rg_w_out, hg_w_in, hg_lower_bounds,
           hg_gnorm, hg_w_out, mlp_w1, mlp_w2):
    depth = mod_w.shape[0]
    bsz, t, d = x.shape
    mod = _mod_call(c, mod_w, mod_b).reshape(depth, bsz, 6, d)
    row = lambda p: p.reshape(1, -1)
    for layer in range(depth):
        j = layer // 2
        if layer % 2 == 0:
            wg = jnp.concatenate([rg_w_a[j], rg_w_x[j]], axis=-1).astype(_BF16)
            x = _rg_call(x, mod[layer], row(norm_mix[layer]), rg_w_in[j].astype(_BF16),
                         rg_conv_w[j], row(rg_conv_b[j]), wg, row(rg_b_a[j]), row(rg_b_x[j]),
                         row(rg_lambda[j]), rg_w_out[j].astype(_BF16))
        else:
            x = _hg_call(x, mod[layer], row(norm_mix[layer]), hg_w_in[j].astype(_BF16),
                         hg_lower_bounds, row(hg_gnorm[j]), hg_w_out[j].astype(_BF16),
                         layer=layer)
        x = _mlp_call(x, mod[layer], row(norm_mlp[layer]), mlp_w1[layer].astype(_BF16),
                      mlp_w2[layer].astype(_BF16), row(norm_final),
                      final=(layer == depth - 1))
    return x
```

```python
import functools

import jax
import jax.numpy as jnp
from jax import lax
from jax.experimental import pallas as pl
from jax.experimental.pallas import tpu as pltpu

NORM_EPS = 1e-6
GNORM_EPS = 1e-5
LRU_C = 8.0
LOG2_E = 1.4426950408889634
N_HEADS = 8
HEAD_DIM = 128
SUBLANES = 8
HG_CHUNK = 128
HG_DIAG = 32
VMEM_LIMIT = 56 * 1024 * 1024

_BF16 = jnp.bfloat16
_F32 = jnp.float32
_NT = (((1,), (1,)), ((), ()))
_TN = (((0,), (0,)), ((), ()))


def _dot(a, b):
    return jnp.dot(a, b, preferred_element_type=_F32)


def _sigmoid(x):
    return 0.5 * jnp.tanh(0.5 * x) + 0.5


def _silu(x):
    return x * _sigmoid(x)


def _gelu_tanh(x):
    return 0.5 * x * (1.0 + jnp.tanh(0.7978845608028654 * (x + 0.044715 * (x * x * x))))


def _rms_norm(x, gain, eps):
    ms = jnp.mean(x * x, axis=-1, keepdims=True)
    return x * lax.rsqrt(ms + eps) * gain


def _norm_mod(x, gain, scale, shift):
    return _rms_norm(x, gain, NORM_EPS) * (1.0 + scale) + shift


def _mod_kernel(c_ref, w_ref, b_ref, o_ref):
    cs = _silu(c_ref[...]).astype(_BF16)
    o_ref[...] = _dot(cs, w_ref[...].astype(_BF16)) + b_ref[...]


def _mod_call(c, mod_w, mod_b):
    depth, d, n6 = mod_w.shape
    bsz = c.shape[0]
    tn = 2048
    return pl.pallas_call(
        _mod_kernel,
        grid=(depth, n6 // tn),
        in_specs=[
            pl.BlockSpec((bsz, d), lambda l, j: (0, 0)),
            pl.BlockSpec((None, d, tn), lambda l, j: (l, 0, j)),
            pl.BlockSpec((None, 1, tn), lambda l, j: (l, 0, j)),
        ],
        out_specs=pl.BlockSpec((None, bsz, tn), lambda l, j: (l, 0, j)),
        out_shape=jax.ShapeDtypeStruct((depth, bsz, n6), _F32),
        compiler_params=pltpu.CompilerParams(
            dimension_semantics=("arbitrary", "arbitrary"),
            vmem_limit_bytes=VMEM_LIMIT),
        name="adaln_mod",
    )(c, mod_w, mod_b.reshape(depth, 1, n6))


def _mlp_kernel(x_ref, mod_ref, gain_ref, w1_ref, w2_ref, gf_ref, o_ref, hid_ref, *, final, fc):
    x = x_ref[...]
    m = mod_ref[...]
    h = _norm_mod(x, gain_ref[...], m[4:5], m[3:4]).astype(_BF16)
    d_ff = w1_ref.shape[1]
    for c in range(d_ff // fc):
        t = jnp.maximum(_dot(h, w1_ref[:, c * fc:(c + 1) * fc]), 0.0)
        hid_ref[:, c * fc:(c + 1) * fc] = (t * t).astype(_BF16)
    y = _dot(hid_ref[...], w2_ref[...])
    xn = x + m[5:6] * y
    if final:
        xn = _rms_norm(xn, gf_ref[...], NORM_EPS)
    o_ref[...] = xn


def _mlp_call(x, mod_l, gain, w1, w2, gain_final, *, final, tm=1024, fc=1024):
    bsz, t, d = x.shape
    d_ff = w1.shape[1]
    nt = t // tm
    const = lambda b, i: (0, 0)
    return pl.pallas_call(
        functools.partial(_mlp_kernel, final=final, fc=fc),
        grid=(bsz, nt),
        in_specs=[
            pl.BlockSpec((None, tm, d), lambda b, i: (b, i, 0)),
            pl.BlockSpec((None, 6, d), lambda b, i: (b, 0, 0)),
            pl.BlockSpec((1, d), const),
            pl.BlockSpec((d, d_ff), const, pipeline_mode=pl.Buffered(1)),
            pl.BlockSpec((d_ff, d), const, pipeline_mode=pl.Buffered(1)),
            pl.BlockSpec((1, d), const),
        ],
        out_specs=pl.BlockSpec((None, tm, d), lambda b, i: (b, i, 0)),
        out_shape=jax.ShapeDtypeStruct((bsz, t, d), _F32),
        scratch_shapes=[pltpu.VMEM((tm, d_ff), _BF16)],
        compiler_params=pltpu.CompilerParams(
            dimension_semantics=("arbitrary", "arbitrary"),
            vmem_limit_bytes=VMEM_LIMIT),
        name="mlp_final" if final else "mlp",
    )(x, mod_l, gain, w1, w2, gain_final)


def _shift_rows_one(x, first_row):
    row = lax.broadcasted_iota(jnp.int32, x.shape, 0)
    return jnp.where(row == 0, first_row, pltpu.roll(x, 1, 0))


def _rg_kernel(x_ref, mod_ref, gain_ref, win_ref, cw_ref, cb_ref, wg_ref, ba_ref, bx_ref,
               lam_ref, wout_ref, o_ref, a_ref, u_ref, gate_ref, prev_ref, hc_ref):
    nj, d8 = x_ref.shape
    d = d8 // SUBLANES
    tb = nj * SUBLANES
    tblk = pl.program_id(1)
    slab = lambda p: slice(p * nj, (p + 1) * nj)

    @pl.when(tblk == 0)
    def _():
        prev_ref[...] = jnp.zeros_like(prev_ref)
        hc_ref[...] = jnp.zeros_like(hc_ref)

    m = mod_ref[...]
    x = jnp.concatenate([x_ref[:, p * d:(p + 1) * d] for p in range(SUBLANES)], axis=0)
    h = _norm_mod(x, gain_ref[...], m[1:2], m[0:1]).astype(_BF16)
    gate_ref[...] = _gelu_tanh(_dot(h, win_ref[:, d:]))
    xb = _dot(h, win_ref[:, :d])

    cw = cw_ref[...]
    n_tap = cw.shape[0]
    back = {}
    for p in range(SUBLANES - (n_tap - 1), SUBLANES):
        back[p - SUBLANES] = _shift_rows_one(xb[slab(p), :], prev_ref[p:p + 1, :])
    for p in range(SUBLANES):
        prev_ref[p:p + 1, :] = xb[(p + 1) * nj - 1:(p + 1) * nj, :]
    conv = []
    for p in range(SUBLANES):
        acc = cb_ref[...] + xb[slab(p), :] * cw[n_tap - 1:n_tap, :]
        for k in range(1, n_tap):
            src = xb[slab(p - k), :] if p - k >= 0 else back[p - k]
            acc = acc + src * cw[n_tap - 1 - k:n_tap - k, :]
        conv.append(acc)
    xc = jnp.concatenate(conv, axis=0)

    lam = lam_ref[...]
    nl = -lam
    softplus = jnp.maximum(nl, 0.0) + jnp.log1p(jnp.exp(-jnp.abs(nl)))
    rate = (-LRU_C * LOG2_E) * softplus
    first_row = jnp.logical_and(
        lax.broadcasted_iota(jnp.int32, (tb, HEAD_DIM), 0) == 0, tblk == 0)
    for hd in range(N_HEADS):
        sl = slice(hd * HEAD_DIM, (hd + 1) * HEAD_DIM)
        xh = xc[:, sl]
        gp = _dot(xh.astype(_BF16), wg_ref[hd])
        r = _sigmoid(gp[:, :HEAD_DIM] + ba_ref[:, sl])
        i = _sigmoid(gp[:, HEAD_DIM:] + bx_ref[:, sl])
        a = jnp.exp2(r * rate[:, sl])
        y = 1.0 - a * a
        mult = jnp.where(first_row, 1.0, jnp.where(y > 0.0, y * lax.rsqrt(y), 0.0))
        a_ref[:, sl] = a
        u_ref[:, sl] = mult * (i * xh)

    hl = u_ref[slab(0), :]
    pr = a_ref[slab(0), :]
    for p in range(1, SUBLANES):
        ap = a_ref[slab(p), :]
        hl = ap * hl + u_ref[slab(p), :]
        pr = ap * pr
        u_ref[slab(p), :] = hl
        a_ref[slab(p), :] = pr
    sub = lax.broadcasted_iota(jnp.int32, (SUBLANES, d), 0)
    carry = hc_ref[...]
    cin = []
    for g in range(nj // SUBLANES):
        rows = slice(g * SUBLANES, (g + 1) * SUBLANES)
        u = hl[rows, :]
        a = pr[rows, :]
        for shift in (1, 2, 4):
            keep = sub >= shift
            u = u + a * jnp.where(keep, pltpu.roll(u, shift, 0), 0.0)
            a = a * jnp.where(keep, pltpu.roll(a, shift, 0), 1.0)
        cout = u + a * carry
        cin.append(jnp.where(sub == 0, carry, pltpu.roll(cout, 1, 0)))
        carry = jnp.broadcast_to(cout[SUBLANES - 1:SUBLANES, :], (SUBLANES, d))
    hc_ref[...] = carry
    cin = jnp.concatenate(cin, axis=0)
    hs = [u_ref[slab(p), :] + a_ref[slab(p), :] * cin for p in range(SUBLANES)]
    z = (jnp.concatenate(hs, axis=0) * gate_ref[...]).astype(_BF16)
    out = x + m[2:3] * _dot(z, wout_ref[...])
    for p in range(SUBLANES):
        o_ref[:, p * d:(p + 1) * d] = out[slab(p), :]


def _rg_call(x, mod_l, gain, w_in, conv_w, conv_b, wg, b_a, b_x, lam, w_out, *, tb=512):
    bsz, t, d = x.shape
    nj = tb // SUBLANES
    const2 = lambda b, i: (0, 0)
    const3 = lambda b, i: (0, 0, 0)
    xv = x.reshape(bsz, t // SUBLANES, SUBLANES * d)
    out = pl.pallas_call(
        _rg_kernel,
        grid=(bsz, t // tb),
        in_specs=[
            pl.BlockSpec((None, nj, SUBLANES * d), lambda b, i: (b, i, 0)),
            pl.BlockSpec((None, 6, d), lambda b, i: (b, 0, 0)),
            pl.BlockSpec((1, d), const2),
            pl.BlockSpec(w_in.shape, const2, pipeline_mode=pl.Buffered(1)),
            pl.BlockSpec(conv_w.shape, const2),
            pl.BlockSpec((1, d), const2),
            pl.BlockSpec(wg.shape, const3, pipeline_mode=pl.Buffered(1)),
            pl.BlockSpec((1, d), const2),
            pl.BlockSpec((1, d), const2),
            pl.BlockSpec((1, d), const2),
            pl.BlockSpec(w_out.shape, const2, pipeline_mode=pl.Buffered(1)),
        ],
        out_specs=pl.BlockSpec((None, nj, SUBLANES * d), lambda b, i: (b, i, 0)),
        out_shape=jax.ShapeDtypeStruct(xv.shape, _F32),
        scratch_shapes=[
            pltpu.VMEM((tb, d), _F32),
            pltpu.VMEM((tb, d), _F32),
            pltpu.VMEM((tb, d), _F32),
            pltpu.VMEM((SUBLANES, d), _F32),
            pltpu.VMEM((SUBLANES, d), _F32),
        ],
        compiler_params=pltpu.CompilerParams(
            dimension_semantics=("arbitrary", "arbitrary"),
            vmem_limit_bytes=VMEM_LIMIT),
        name="rglru",
    )(xv, mod_l, gain, w_in, conv_w, conv_b, wg, b_a, b_x, lam, w_out)
    return out.reshape(bsz, t, d)


def _block_row_bcast(b, block, row):
    n, d = b.shape
    parts = []
    for s in range(0, n, block):
        parts.append(jnp.broadcast_to(b[s + row:s + row + 1, :], (block, d)))
    return jnp.concatenate(parts, axis=0)


def _hg_lower_bound(lbp_ref, layer):
    lbp = lbp_ref[...]
    depth = lbp.shape[0]
    mx = lbp[0:1]
    for j in range(1, depth):
        mx = jnp.maximum(mx, lbp[j:j + 1])
    es = [jnp.exp(lbp[j:j + 1] - mx) for j in range(depth)]
    tot = es[0]
    for j in range(1, depth):
        tot = tot + es[j]
    num = es[1]
    for j in range(2, layer + 1):
        num = num + es[j]
    return num / tot


class _HgProducer:
    def __init__(self, x_ref, mod_ref, gain_ref, win_ref, lbp_ref, slot, h_ref, layer):
        self.x_ref, self.win_ref, self.h_ref = x_ref, win_ref, h_ref
        self.q_ref, self.k_ref, self.b_ref, self.v_ref, self.g_ref = slot
        self.tb, self.d = x_ref.shape
        self.lb = _hg_lower_bound(lbp_ref, layer)
        m = mod_ref[...]
        self.scale = gain_ref[...] * (1.0 + m[1:2])
        self.shift = m[0:1]
        cc = HG_CHUNK
        row = lax.broadcasted_iota(jnp.int32, (cc, cc), 0)
        col = lax.broadcasted_iota(jnp.int32, (cc, cc), 1)
        tril = jnp.where(row >= col, 1.0, 0.0).astype(_BF16)
        self.tril2 = jnp.concatenate([tril, tril], axis=1)

    def norm(self):
        x = self.x_ref[...]
        ms = jnp.mean(x * x, axis=-1, keepdims=True)
        self.h_ref[...] = (x * lax.rsqrt(ms + NORM_EPS) * self.scale + self.shift).astype(_BF16)

    def _proj(self, part, rows, cols):
        w = self.win_ref[:, part * self.d + cols.start:part * self.d + cols.stop]
        return _dot(self.h_ref[rows, :], w)

    def query(self, rows, cols):
        self.q_ref[rows, cols] = _silu(self._proj(0, rows, cols))

    def forget(self, rows, cols):
        lb = self.lb[:, cols]
        fg = lb + (1.0 - lb) * _sigmoid(self._proj(1, rows, cols))
        self.k_ref[rows, cols] = 1.0 - fg
        lf = jnp.log2(fg)
        p0 = lf.astype(_BF16)
        p1 = (lf - p0.astype(_F32)).astype(_BF16)
        for r in range(0, rows.stop - rows.start, HG_CHUNK):
            self.b_ref[rows.start + r:rows.start + r + HG_CHUNK, cols] = _dot(
                self.tril2, jnp.concatenate([p0[r:r + HG_CHUNK], p1[r:r + HG_CHUNK]], axis=0))

    def value(self, rows, cols):
        self.v_ref[rows, cols] = self._proj(2, rows, cols).astype(_BF16)

    def gate(self, rows, cols):
        self.g_ref[rows, cols] = _silu(self._proj(3, rows, cols))


class _HgConsumer:
    def __init__(self, x_ref, mod_ref, gn_ref, wout_ref, slot, st_ref, z_ref, o_ref):
        self.x_ref, self.wout_ref, self.st_ref = x_ref, wout_ref, st_ref
        self.z_ref, self.o_ref = z_ref, o_ref
        self.q_ref, self.k_ref, self.b_ref, self.v_ref, self.g_ref = slot
        self.res_gate = mod_ref[2:3, :]
        self.gn = gn_ref[...]
        cc = HG_CHUNK
        row = lax.broadcasted_iota(jnp.int32, (cc, cc), 0)
        col = lax.broadcasted_iota(jnp.int32, (cc, cc), 1)
        self.levels = []
        s = cc // 2
        while s >= HG_DIAG:
            same = (row // (2 * s)) == (col // (2 * s))
            self.levels.append((s, same & ((row % (2 * s)) >= s) & ((col % (2 * s)) < s)))
            s //= 2
        self.diag_mask = ((row // HG_DIAG) == (col // HG_DIAG)) & (col <= row)
        self.outs = None

    def prep(self, rows):
        cc = HG_CHUNK
        q = self.q_ref[rows, :]
        k = self.k_ref[rows, :]
        b = self.b_ref[rows, :]
        b_last = b[cc - 1:cc, :]
        self.v = self.v_ref[rows, :]
        self.qe = (q * jnp.exp2(b)).astype(_BF16)
        self.ke = (k * jnp.exp2(b_last - b)).astype(_BF16)
        self.dec = jnp.exp2(b_last)
        self.facs = []
        for s, _ in self.levels:
            ref = _block_row_bcast(b, 2 * s, s)
            self.facs.append(((q * jnp.exp2(b - ref)).astype(_BF16),
                              (k * jnp.exp2(ref - b)).astype(_BF16)))
        ref = _block_row_bcast(b, HG_DIAG, HG_DIAG // 2)
        self.qd = (q * jnp.exp2(b - ref)).astype(_BF16)
        self.kd = (k * jnp.exp2(ref - b)).astype(_BF16)
        self.outs = []

    def scores(self, hd):
        sl = slice(hd * HEAD_DIM, (hd + 1) * HEAD_DIM)
        att = jnp.where(self.diag_mask, lax.dot_general(
            self.qd[:, sl], self.kd[:, sl], _NT, preferred_element_type=_F32), 0.0)
        for (s, mask), (qf, kf) in zip(self.levels, self.facs):
            att = jnp.where(mask, lax.dot_general(
                qf[:, sl], kf[:, sl], _NT, preferred_element_type=_F32), att)
        return att.astype(_BF16)

    def head(self, hd, att):
        sl = slice(hd * HEAD_DIM, (hd + 1) * HEAD_DIM)
        st = self.st_ref[hd]
        o = _dot(att, self.v[:, sl]) + lax.dot_general(
            self.qe[:, sl], st.astype(_BF16), _NT, preferred_element_type=_F32)
        self.st_ref[hd] = st * self.dec[:, sl] + lax.dot_general(
            self.v[:, sl], self.ke[:, sl], _TN, preferred_element_type=_F32)
        self.outs.append(_rms_norm(o, self.gn, GNORM_EPS))

    def chunk(self, rows):
        self.prep(rows)
        atts = [self.scores(hd) for hd in range(N_HEADS)]
        for hd in range(N_HEADS):
            self.head(hd, atts[hd])
        self.finish(rows)

    def finish(self, rows):
        self.z_ref[rows, :] = (
            jnp.concatenate(self.outs, axis=1) * self.g_ref[rows, :]).astype(_BF16)

    def out_proj(self, rows, cols):
        y = _dot(self.z_ref[rows, :], self.wout_ref[:, cols])
        self.o_ref[rows, cols] = self.x_ref[rows, cols] + self.res_gate[:, cols] * y


def _hg_kernel(xp_ref, modp_ref, xc_ref, modc_ref, gain_ref, win_ref, lbp_ref, gn_ref, wout_ref,
               o_ref, *scratch, layer, nt_blocks):
    slots = (scratch[0:5], scratch[5:10])
    st_ref, h_ref, z_ref = scratch[10:13]
    tb, d = xp_ref.shape
    s = pl.program_id(0)

    @pl.when(s == 0)
    def _():
        for r in slots[1]:
            r[...] = jnp.zeros_like(r)

    @pl.when(jnp.logical_or(s == 0, (s - 1) % nt_blocks == 0))
    def _():
        st_ref[...] = jnp.zeros_like(st_ref)

    def step(p_slot, c_slot):
        prod = _HgProducer(xp_ref, modp_ref, gain_ref, win_ref, lbp_ref, p_slot, h_ref, layer)
        cons = _HgConsumer(xc_ref, modc_ref, gn_ref, wout_ref, c_slot, st_ref, z_ref, o_ref)
        block, full = slice(0, tb), slice(0, d)
        prod.norm()
        for piece in (prod.query, prod.forget, prod.value, prod.gate):
            piece(block, full)
        for c in range(tb // HG_CHUNK):
            cons.chunk(slice(c * HG_CHUNK, (c + 1) * HG_CHUNK))
        cons.out_proj(block, full)

    @pl.when(s % 2 == 0)
    def _():
        step(slots[0], slots[1])

    @pl.when(s % 2 == 1)
    def _():
        step(slots[1], slots[0])


def _hg_call(x, mod_l, gain, w_in, lbp, gnorm, w_out, *, layer, tb=512):
    bsz, t, d = x.shape
    ntb = t // tb
    n_blocks = bsz * ntb
    const2 = lambda s: (0, 0)
    p_blk = lambda s: jnp.minimum(s, n_blocks - 1)
    c_blk = lambda s: jnp.maximum(s - 1, 0)
    x_p = lambda s: (p_blk(s) // ntb, p_blk(s) % ntb, 0)
    x_c = lambda s: (c_blk(s) // ntb, c_blk(s) % ntb, 0)
    slot = [pltpu.VMEM((tb, d), _F32), pltpu.VMEM((tb, d), _F32), pltpu.VMEM((tb, d), _F32),
            pltpu.VMEM((tb, d), _BF16), pltpu.VMEM((tb, d), _F32)]
    return pl.pallas_call(
        functools.partial(_hg_kernel, layer=layer, nt_blocks=ntb),
        grid=(n_blocks + 1,),
        in_specs=[
            pl.BlockSpec((None, tb, d), x_p),
            pl.BlockSpec((None, 6, d), lambda s: (p_blk(s) // ntb, 0, 0)),
            pl.BlockSpec((None, tb, d), x_c),
            pl.BlockSpec((None, 6, d), lambda s: (c_blk(s) // ntb, 0, 0)),
            pl.BlockSpec((1, d), const2),
            pl.BlockSpec(w_in.shape, const2, pipeline_mode=pl.Buffered(1)),
            pl.BlockSpec(lbp.shape, const2),
            pl.BlockSpec((1, HEAD_DIM), const2),
            pl.BlockSpec(w_out.shape, const2, pipeline_mode=pl.Buffered(1)),
        ],
        out_specs=pl.BlockSpec((None, tb, d), x_c),
        out_shape=jax.ShapeDtypeStruct((bsz, t, d), _F32),
        scratch_shapes=slot + slot + [
            pltpu.VMEM((N_HEADS, HEAD_DIM, HEAD_DIM), _F32),
            pltpu.VMEM((tb, d), _BF16),
            pltpu.VMEM((tb, d), _BF16)],
        compiler_params=pltpu.CompilerParams(
            dimension_semantics=("arbitrary",),
            vmem_limit_bytes=VMEM_LIMIT),
        name="hgrn2",
    )(x, mod_l, x, mod_l, gain, w_in, lbp, gnorm, w_out)


def kernel(x, c, mod_w, mod_b, norm_mix, norm_mlp, norm_final, rg_w_in, rg_conv_w, rg_conv_b,
           rg_w_a, rg_b_a, rg_w_x, rg_b_x, rg_lambda, rg_w_out, hg_w_in, hg_lower_bounds,
           hg_gnorm, hg_w_out, mlp_w1, mlp_w2):
    depth = mod_w.shape[0]
    bsz, t, d = x.shape
    mod = _mod_call(c, mod_w, mod_b).reshape(depth, bsz, 6, d)
    row = lambda p: p.reshape(1, -1)
    for layer in range(depth):
        j = layer // 2
        if layer % 2 == 0:
            wg = jnp.concatenate([rg_w_a[j], rg_w_x[j]], axis=-1).astype(_BF16)
            x = _rg_call(x, mod[layer], row(norm_mix[layer]), rg_w_in[j].astype(_BF16),
                         rg_conv_w[j], row(rg_conv_b[j]), wg, row(rg_b_a[j]), row(rg_b_x[j]),
                         row(rg_lambda[j]), rg_w_out[j].astype(_BF16))
        else:
            x = _hg_call(x, mod[layer], row(norm_mix[layer]), hg_w_in[j].astype(_BF16),
                         hg_lower_bounds, row(hg_gnorm[j]), hg_w_out[j].astype(_BF16),
                         layer=layer)
        x = _mlp_call(x, mod[layer], row(norm_mlp[layer]), mlp_w1[layer].astype(_BF16),
                      mlp_w2[layer].astype(_BF16), row(norm_final),
                      final=(layer == depth - 1))
    return x
```

```python
import functools

import jax
import jax.numpy as jnp
from jax import lax
from jax.experimental import pallas as pl
from jax.experimental.pallas import tpu as pltpu

NORM_EPS = 1e-6
GNORM_EPS = 1e-5
LRU_C = 8.0
LOG2_E = 1.4426950408889634
N_HEADS = 8
HEAD_DIM = 128
SUBLANES = 8
HG_CHUNK = 128
HG_DIAG = 32
VMEM_LIMIT = 56 * 1024 * 1024

_BF16 = jnp.bfloat16
_F32 = jnp.float32
_NT = (((1,), (1,)), ((), ()))
_TN = (((0,), (0,)), ((), ()))


def _dot(a, b):
    return jnp.dot(a, b, preferred_element_type=_F32)


def _sigmoid(x):
    return 0.5 * jnp.tanh(0.5 * x) + 0.5


def _silu(x):
    return x * _sigmoid(x)


def _gelu_tanh(x):
    return 0.5 * x * (1.0 + jnp.tanh(0.7978845608028654 * (x + 0.044715 * (x * x * x))))


def _rms_norm(x, gain, eps):
    ms = jnp.mean(x * x, axis=-1, keepdims=True)
    return x * lax.rsqrt(ms + eps) * gain


def _norm_mod(x, gain, scale, shift):
    return _rms_norm(x, gain, NORM_EPS) * (1.0 + scale) + shift


def _mod_kernel(c_ref, w_ref, b_ref, o_ref):
    cs = _silu(c_ref[...]).astype(_BF16)
    o_ref[...] = _dot(cs, w_ref[...].astype(_BF16)) + b_ref[...]


def _mod_call(c, mod_w, mod_b):
    depth, d, n6 = mod_w.shape
    bsz = c.shape[0]
    tn = 2048
    return pl.pallas_call(
        _mod_kernel,
        grid=(depth, n6 // tn),
        in_specs=[
            pl.BlockSpec((bsz, d), lambda l, j: (0, 0)),
            pl.BlockSpec((None, d, tn), lambda l, j: (l, 0, j)),
            pl.BlockSpec((None, 1, tn), lambda l, j: (l, 0, j)),
        ],
        out_specs=pl.BlockSpec((None, bsz, tn), lambda l, j: (l, 0, j)),
        out_shape=jax.ShapeDtypeStruct((depth, bsz, n6), _F32),
        compiler_params=pltpu.CompilerParams(
            dimension_semantics=("arbitrary", "arbitrary"),
            vmem_limit_bytes=VMEM_LIMIT),
        name="adaln_mod",
    )(c, mod_w, mod_b.reshape(depth, 1, n6))


def _mlp_kernel(x_ref, mod_ref, gain_ref, w1_ref, w2_ref, gf_ref, o_ref, hid_ref, *, final, fc):
    x = x_ref[...]
    m = mod_ref[...]
    h = _norm_mod(x, gain_ref[...], m[4:5], m[3:4]).astype(_BF16)
    d_ff = w1_ref.shape[1]
    for c in range(d_ff // fc):
        t = jnp.maximum(_dot(h, w1_ref[:, c * fc:(c + 1) * fc]), 0.0)
        hid_ref[:, c * fc:(c + 1) * fc] = (t * t).astype(_BF16)
    y = _dot(hid_ref[...], w2_ref[...])
    xn = x + m[5:6] * y
    if final:
        xn = _rms_norm(xn, gf_ref[...], NORM_EPS)
    o_ref[...] = xn


def _mlp_call(x, mod_l, gain, w1, w2, gain_final, *, final, tm=1024, fc=1024):
    bsz, t, d = x.shape
    d_ff = w1.shape[1]
    nt = t // tm
    const = lambda b, i: (0, 0)
    return pl.pallas_call(
        functools.partial(_mlp_kernel, final=final, fc=fc),
        grid=(bsz, nt),
        in_specs=[
            pl.BlockSpec((None, tm, d), lambda b, i: (b, i, 0)),
            pl.BlockSpec((None, 6, d), lambda b, i: (b, 0, 0)),
            pl.BlockSpec((1, d), const),
            pl.BlockSpec((d, d_ff), const, pipeline_mode=pl.Buffered(1)),
            pl.BlockSpec((d_ff, d), const, pipeline_mode=pl.Buffered(1)),
            pl.BlockSpec((1, d), const),
        ],
        out_specs=pl.BlockSpec((None, tm, d), lambda b, i: (b, i, 0)),
        out_shape=jax.ShapeDtypeStruct((bsz, t, d), _F32),
        scratch_shapes=[pltpu.VMEM((tm, d_ff), _BF16)],
        compiler_params=pltpu.CompilerParams(
            dimension_semantics=("arbitrary", "arbitrary"),
            vmem_limit_bytes=VMEM_LIMIT),
        name="mlp_final" if final else "mlp",
    )(x, mod_l, gain, w1, w2, gain_final)


def _shift_rows_one(x, first_row):
    row = lax.broadcasted_iota(jnp.int32, x.shape, 0)
    return jnp.where(row == 0, first_row, pltpu.roll(x, 1, 0))


def _rg_kernel(x_ref, mod_ref, gain_ref, win_ref, cw_ref, cb_ref, wg_ref, ba_ref, bx_ref,
               lam_ref, wout_ref, o_ref, a_ref, u_ref, gate_ref, prev_ref, hc_ref):
    nj, _, d = x_ref.shape
    tb = nj * SUBLANES
    tblk = pl.program_id(1)
    slab = lambda p: slice(p * nj, (p + 1) * nj)

    @pl.when(tblk == 0)
    def _():
        prev_ref[...] = jnp.zeros_like(prev_ref)
        hc_ref[...] = jnp.zeros_like(hc_ref)

    m = mod_ref[...]
    x = jnp.concatenate([x_ref[:, p, :] for p in range(SUBLANES)], axis=0)
    h = _norm_mod(x, gain_ref[...], m[1:2], m[0:1]).astype(_BF16)
    gate_ref[...] = _gelu_tanh(_dot(h, win_ref[:, d:]))
    xb = _dot(h, win_ref[:, :d])

    cw = cw_ref[...]
    n_tap = cw.shape[0]
    back = {}
    for p in range(SUBLANES - (n_tap - 1), SUBLANES):
        back[p - SUBLANES] = _shift_rows_one(xb[slab(p), :], prev_ref[p:p + 1, :])
    for p in range(SUBLANES):
        prev_ref[p:p + 1, :] = xb[(p + 1) * nj - 1:(p + 1) * nj, :]
    conv = []
    for p in range(SUBLANES):
        acc = cb_ref[...] + xb[slab(p), :] * cw[n_tap - 1:n_tap, :]
        for k in range(1, n_tap):
            src = xb[slab(p - k), :] if p - k >= 0 else back[p - k]
            acc = acc + src * cw[n_tap - 1 - k:n_tap - k, :]
        conv.append(acc)
    xc = jnp.concatenate(conv, axis=0)

    lam = lam_ref[...]
    nl = -lam
    softplus = jnp.maximum(nl, 0.0) + jnp.log1p(jnp.exp(-jnp.abs(nl)))
    rate = (-LRU_C * LOG2_E) * softplus
    first_row = jnp.logical_and(
        lax.broadcasted_iota(jnp.int32, (tb, HEAD_DIM), 0) == 0, tblk == 0)
    for hd in range(N_HEADS):
        sl = slice(hd * HEAD_DIM, (hd + 1) * HEAD_DIM)
        xh = xc[:, sl]
        gp = _dot(xh.astype(_BF16), wg_ref[hd])
        r = _sigmoid(gp[:, :HEAD_DIM] + ba_ref[:, sl])
        i = _sigmoid(gp[:, HEAD_DIM:] + bx_ref[:, sl])
        a = jnp.exp2(r * rate[:, sl])
        y = 1.0 - a * a
        mult = jnp.where(first_row, 1.0, jnp.where(y > 0.0, y * lax.rsqrt(y), 0.0))
        a_ref[:, sl] = a
        u_ref[:, sl] = mult * (i * xh)

    hl = u_ref[slab(0), :]
    pr = a_ref[slab(0), :]
    for p in range(1, SUBLANES):
        ap = a_ref[slab(p), :]
        hl = ap * hl + u_ref[slab(p), :]
        pr = ap * pr
        u_ref[slab(p), :] = hl
        a_ref[slab(p), :] = pr
    sub = lax.broadcasted_iota(jnp.int32, (SUBLANES, d), 0)
    carry = hc_ref[...]
    cin = []
    for g in range(nj // SUBLANES):
        rows = slice(g * SUBLANES, (g + 1) * SUBLANES)
        u = hl[rows, :]
        a = pr[rows, :]
        for shift in (1, 2, 4):
            keep = sub >= shift
            u = u + a * jnp.where(keep, pltpu.roll(u, shift, 0), 0.0)
            a = a * jnp.where(keep, pltpu.roll(a, shift, 0), 1.0)
        cout = u + a * carry
        cin.append(jnp.where(sub == 0, carry, pltpu.roll(cout, 1, 0)))
        carry = jnp.broadcast_to(cout[SUBLANES - 1:SUBLANES, :], (SUBLANES, d))
    hc_ref[...] = carry
    cin = jnp.concatenate(cin, axis=0)
    hs = [u_ref[slab(p), :] + a_ref[slab(p), :] * cin for p in range(SUBLANES)]
    z = (jnp.concatenate(hs, axis=0) * gate_ref[...]).astype(_BF16)
    out = x + m[2:3] * _dot(z, wout_ref[...])
    for p in range(SUBLANES):
        o_ref[:, p, :] = out[slab(p), :]


def _rg_call(x, mod_l, gain, w_in, conv_w, conv_b, wg, b_a, b_x, lam, w_out, *, tb=512):
    bsz, t, d = x.shape
    nj = tb // SUBLANES
    const2 = lambda b, i: (0, 0)
    const3 = lambda b, i: (0, 0, 0)
    xv = x.reshape(bsz, t // SUBLANES, SUBLANES, d)
    out = pl.pallas_call(
        _rg_kernel,
        grid=(bsz, t // tb),
        in_specs=[
            pl.BlockSpec((None, nj, SUBLANES, d), lambda b, i: (b, i, 0, 0)),
            pl.BlockSpec((None, 6, d), lambda b, i: (b, 0, 0)),
            pl.BlockSpec((1, d), const2),
            pl.BlockSpec(w_in.shape, const2, pipeline_mode=pl.Buffered(1)),
            pl.BlockSpec(conv_w.shape, const2),
            pl.BlockSpec((1, d), const2),
            pl.BlockSpec(wg.shape, const3, pipeline_mode=pl.Buffered(1)),
            pl.BlockSpec((1, d), const2),
            pl.BlockSpec((1, d), const2),
            pl.BlockSpec((1, d), const2),
            pl.BlockSpec(w_out.shape, const2, pipeline_mode=pl.Buffered(1)),
        ],
        out_specs=pl.BlockSpec((None, nj, SUBLANES, d), lambda b, i: (b, i, 0, 0)),
        out_shape=jax.ShapeDtypeStruct(xv.shape, _F32),
        scratch_shapes=[
            pltpu.VMEM((tb, d), _F32),
            pltpu.VMEM((tb, d), _F32),
            pltpu.VMEM((tb, d), _F32),
            pltpu.VMEM((SUBLANES, d), _F32),
            pltpu.VMEM((SUBLANES, d), _F32),
        ],
        compiler_params=pltpu.CompilerParams(
            dimension_semantics=("arbitrary", "arbitrary"),
            vmem_limit_bytes=VMEM_LIMIT),
        name="rglru",
    )(xv, mod_l, gain, w_in, conv_w, conv_b, wg, b_a, b_x, lam, w_out)
    return out.reshape(bsz, t, d)


def _block_row_bcast(b, block, row):
    n, d = b.shape
    parts = []
    for s in range(0, n, block):
        parts.append(jnp.broadcast_to(b[s + row:s + row + 1, :], (block, d)))
    return jnp.concatenate(parts, axis=0)


def _hg_lower_bound(lbp_ref, layer):
    lbp = lbp_ref[...]
    depth = lbp.shape[0]
    mx = lbp[0:1]
    for j in range(1, depth):
        mx = jnp.maximum(mx, lbp[j:j + 1])
    es = [jnp.exp(lbp[j:j + 1] - mx) for j in range(depth)]
    tot = es[0]
    for j in range(1, depth):
        tot = tot + es[j]
    num = es[1]
    for j in range(2, layer + 1):
        num = num + es[j]
    return num / tot


class _HgProducer:
    def __init__(self, x_ref, mod_ref, gain_ref, win_ref, lbp_ref, slot, h_ref, layer):
        self.x_ref, self.win_ref, self.h_ref = x_ref, win_ref, h_ref
        self.q_ref, self.k_ref, self.b_ref, self.v_ref, self.g_ref = slot
        self.tb, self.d = x_ref.shape
        self.lb = _hg_lower_bound(lbp_ref, layer)
        m = mod_ref[...]
        self.scale = gain_ref[...] * (1.0 + m[1:2])
        self.shift = m[0:1]
        cc = HG_CHUNK
        row = lax.broadcasted_iota(jnp.int32, (cc, cc), 0)
        col = lax.broadcasted_iota(jnp.int32, (cc, cc), 1)
        tril = jnp.where(row >= col, 1.0, 0.0).astype(_BF16)
        self.tril2 = jnp.concatenate([tril, tril], axis=1)

    def norm(self):
        x = self.x_ref[...]
        ms = jnp.mean(x * x, axis=-1, keepdims=True)
        self.h_ref[...] = (x * lax.rsqrt(ms + NORM_EPS) * self.scale + self.shift).astype(_BF16)

    def _proj(self, part, rows, cols):
        w = self.win_ref[:, part * self.d + cols.start:part * self.d + cols.stop]
        return _dot(self.h_ref[rows, :], w)

    def query(self, rows, cols):
        self.q_ref[rows, cols] = _silu(self._proj(0, rows, cols))

    def forget(self, rows, cols):
        lb = self.lb[:, cols]
        fg = lb + (1.0 - lb) * _sigmoid(self._proj(1, rows, cols))
        self.k_ref[rows, cols] = 1.0 - fg
        lf = jnp.log2(fg)
        p0 = lf.astype(_BF16)
        p1 = (lf - p0.astype(_F32)).astype(_BF16)
        for r in range(0, rows.stop - rows.start, HG_CHUNK):
            self.b_ref[rows.start + r:rows.start + r + HG_CHUNK, cols] = _dot(
                self.tril2, jnp.concatenate([p0[r:r + HG_CHUNK], p1[r:r + HG_CHUNK]], axis=0))

    def value(self, rows, cols):
        self.v_ref[rows, cols] = self._proj(2, rows, cols).astype(_BF16)

    def gate(self, rows, cols):
        self.g_ref[rows, cols] = _silu(self._proj(3, rows, cols))


class _HgConsumer:
    def __init__(self, x_ref, mod_ref, gn_ref, wout_ref, slot, st_ref, z_ref, o_ref):
        self.x_ref, self.wout_ref, self.st_ref = x_ref, wout_ref, st_ref
        self.z_ref, self.o_ref = z_ref, o_ref
        self.q_ref, self.k_ref, self.b_ref, self.v_ref, self.g_ref = slot
        self.res_gate = mod_ref[2:3, :]
        self.gn = gn_ref[...]
        cc = HG_CHUNK
        row = lax.broadcasted_iota(jnp.int32, (cc, cc), 0)
        col = lax.broadcasted_iota(jnp.int32, (cc, cc), 1)
        self.levels = []
        s = cc // 2
        while s >= HG_DIAG:
            same = (row // (2 * s)) == (col // (2 * s))
            self.levels.append((s, same & ((row % (2 * s)) >= s) & ((col % (2 * s)) < s)))
            s //= 2
        self.diag_mask = ((row // HG_DIAG) == (col // HG_DIAG)) & (col <= row)
        self.outs = None

    def prep(self, rows):
        cc = HG_CHUNK
        q = self.q_ref[rows, :]
        k = self.k_ref[rows, :]
        b = self.b_ref[rows, :]
        b_last = b[cc - 1:cc, :]
        self.v = self.v_ref[rows, :]
        self.qe = (q * jnp.exp2(b)).astype(_BF16)
        self.ke = (k * jnp.exp2(b_last - b)).astype(_BF16)
        self.dec = jnp.exp2(b_last)
        self.facs = []
        for s, _ in self.levels:
            ref = _block_row_bcast(b, 2 * s, s)
            self.facs.append(((q * jnp.exp2(b - ref)).astype(_BF16),
                              (k * jnp.exp2(ref - b)).astype(_BF16)))
        ref = _block_row_bcast(b, HG_DIAG, HG_DIAG // 2)
        self.qd = (q * jnp.exp2(b - ref)).astype(_BF16)
        self.kd = (k * jnp.exp2(ref - b)).astype(_BF16)
        self.outs = []

    def scores(self, hd):
        sl = slice(hd * HEAD_DIM, (hd + 1) * HEAD_DIM)
        att = jnp.where(self.diag_mask, lax.dot_general(
            self.qd[:, sl], self.kd[:, sl], _NT, preferred_element_type=_F32), 0.0)
        for (s, mask), (qf, kf) in zip(self.levels, self.facs):
            att = jnp.where(mask, lax.dot_general(
                qf[:, sl], kf[:, sl], _NT, preferred_element_type=_F32), att)
        return att.astype(_BF16)

    def head(self, hd, att):
        sl = slice(hd * HEAD_DIM, (hd + 1) * HEAD_DIM)
        st = self.st_ref[hd]
        o = _dot(att, self.v[:, sl]) + lax.dot_general(
            self.qe[:, sl], st.astype(_BF16), _NT, preferred_element_type=_F32)
        self.st_ref[hd] = st * self.dec[:, sl] + lax.dot_general(
            self.v[:, sl], self.ke[:, sl], _TN, preferred_element_type=_F32)
        self.outs.append(_rms_norm(o, self.gn, GNORM_EPS))

    def chunk(self, rows):
        self.prep(rows)
        atts = [self.scores(hd) for hd in range(N_HEADS)]
        for hd in range(N_HEADS):
            self.head(hd, atts[hd])
        self.finish(rows)

    def finish(self, rows):
        self.z_ref[rows, :] = (
            jnp.concatenate(self.outs, axis=1) * self.g_ref[rows, :]).astype(_BF16)

    def out_proj(self, rows, cols):
        y = _dot(self.z_ref[rows, :], self.wout_ref[:, cols])
        self.o_ref[rows, cols] = self.x_ref[rows, cols] + self.res_gate[:, cols] * y


def _hg_kernel(xp_ref, modp_ref, xc_ref, modc_ref, gain_ref, win_ref, lbp_ref, gn_ref, wout_ref,
               o_ref, *scratch, layer, nt_blocks):
    slots = (scratch[0:5], scratch[5:10])
    st_ref, h_ref, z_ref = scratch[10:13]
    tb, d = xp_ref.shape
    s = pl.program_id(0)

    @pl.when(s == 0)
    def _():
        for r in slots[1]:
            r[...] = jnp.zeros_like(r)

    @pl.when(jnp.logical_or(s == 0, (s - 1) % nt_blocks == 0))
    def _():
        st_ref[...] = jnp.zeros_like(st_ref)

    def step(p_slot, c_slot):
        prod = _HgProducer(xp_ref, modp_ref, gain_ref, win_ref, lbp_ref, p_slot, h_ref, layer)
        cons = _HgConsumer(xc_ref, modc_ref, gn_ref, wout_ref, c_slot, st_ref, z_ref, o_ref)
        block, full = slice(0, tb), slice(0, d)
        prod.norm()
        for piece in (prod.query, prod.forget, prod.value, prod.gate):
            piece(block, full)
        for c in range(tb // HG_CHUNK):
            cons.chunk(slice(c * HG_CHUNK, (c + 1) * HG_CHUNK))
        cons.out_proj(block, full)

    @pl.when(s % 2 == 0)
    def _():
        step(slots[0], slots[1])

    @pl.when(s % 2 == 1)
    def _():
        step(slots[1], slots[0])


def _hg_call(x, mod_l, gain, w_in, lbp, gnorm, w_out, *, layer, tb=512):
    bsz, t, d = x.shape
    ntb = t // tb
    n_blocks = bsz * ntb
    const2 = lambda s: (0, 0)
    p_blk = lambda s: jnp.minimum(s, n_blocks - 1)
    c_blk = lambda s: jnp.maximum(s - 1, 0)
    x_p = lambda s: (p_blk(s) // ntb, p_blk(s) % ntb, 0)
    x_c = lambda s: (c_blk(s) // ntb, c_blk(s) % ntb, 0)
    slot = [pltpu.VMEM((tb, d), _F32), pltpu.VMEM((tb, d), _F32), pltpu.VMEM((tb, d), _F32),
            pltpu.VMEM((tb, d), _BF16), pltpu.VMEM((tb, d), _F32)]
    return pl.pallas_call(
        functools.partial(_hg_kernel, layer=layer, nt_blocks=ntb),
        grid=(n_blocks + 1,),
        in_specs=[
            pl.BlockSpec((None, tb, d), x_p),
            pl.BlockSpec((None, 6, d), lambda s: (p_blk(s) // ntb, 0, 0)),
            pl.BlockSpec((None, tb, d), x_c),
            pl.BlockSpec((None, 6, d), lambda s: (c_blk(s) // ntb, 0, 0)),
            pl.BlockSpec((1, d), const2),
            pl.BlockSpec(w_in.shape, const2, pipeline_mode=pl.Buffered(1)),
            pl.BlockSpec(lbp.shape, const2),
            pl.BlockSpec((1, HEAD_DIM), const2),
            pl.BlockSpec(w_out.shape, const2, pipeline_mode=pl.Buffered(1)),
        ],
        out_specs=pl.BlockSpec((None, tb, d), x_c),
        out_shape=jax.ShapeDtypeStruct((bsz, t, d), _F32),
        scratch_shapes=slot + slot + [
            pltpu.VMEM((N_HEADS, HEAD_DIM, HEAD_DIM), _F32),
            pltpu.VMEM((tb, d), _BF16),
            pltpu.VMEM((tb, d), _BF16)],
        compiler_params=pltpu.CompilerParams(
            dimension_semantics=("arbitrary",),
            vmem_limit_bytes=VMEM_LIMIT),
        name="hgrn2",
    )(x, mod_l, x, mod_l, gain, w_in, lbp, gnorm, w_out)


def kernel(x, c, mod_w, mod_b, norm_mix, norm_mlp, norm_final, rg_w_in, rg_conv_w, rg_conv_b,
           rg_w_a, rg_b_a, rg_w_x, rg_b_x, rg_lambda, rg_w_out, hg_w_in, hg_lower_bounds,
           hg_gnorm, hg_w_out, mlp_w1, mlp_w2):
    depth = mod_w.shape[0]
    bsz, t, d = x.shape
    mod = _mod_call(c, mod_w, mod_b).reshape(depth, bsz, 6, d)
    row = lambda p: p.reshape(1, -1)
    for layer in range(depth):
        j = layer // 2
        if layer % 2 == 0:
            wg = jnp.concatenate([rg_w_a[j], rg_w_x[j]], axis=-1).astype(_BF16)
            x = _rg_call(x, mod[layer], row(norm_mix[layer]), rg_w_in[j].astype(_BF16),
                         rg_conv_w[j], row(rg_conv_b[j]), wg, row(rg_b_a[j]), row(rg_b_x[j]),
                         row(rg_lambda[j]), rg_w_out[j].astype(_BF16))
        else:
            x = _hg_call(x, mod[layer], row(norm_mix[layer]), hg_w_in[j].astype(_BF16),
                         hg_lower_bounds, row(hg_gnorm[j]), hg_w_out[j].astype(_BF16),
                         layer=layer)
        x = _mlp_call(x, mod[layer], row(norm_mlp[layer]), mlp_w1[layer].astype(_BF16),
                      mlp_w2[layer].astype(_BF16), row(norm_final),
                      final=(layer == depth - 1))
    return x
```

```python
import functools

import jax
import jax.numpy as jnp
from jax import lax
from jax.experimental import pallas as pl
from jax.experimental.pallas import tpu as pltpu

NORM_EPS = 1e-6
GNORM_EPS = 1e-5
LRU_C = 8.0
LOG2_E = 1.4426950408889634
N_HEADS = 8
HEAD_DIM = 128
SUBLANES = 8
HG_CHUNK = 128
HG_DIAG = 32
VMEM_LIMIT = 56 * 1024 * 1024

_BF16 = jnp.bfloat16
_F32 = jnp.float32
_NT = (((1,), (1,)), ((), ()))
_TN = (((0,), (0,)), ((), ()))


def _dot(a, b):
    return jnp.dot(a, b, preferred_element_type=_F32)


def _sigmoid(x):
    return 0.5 * jnp.tanh(0.5 * x) + 0.5


def _silu(x):
    return x * _sigmoid(x)


def _gelu_tanh(x):
    return 0.5 * x * (1.0 + jnp.tanh(0.7978845608028654 * (x + 0.044715 * (x * x * x))))


def _rms_norm(x, gain, eps):
    ms = jnp.mean(x * x, axis=-1, keepdims=True)
    return x * lax.rsqrt(ms + eps) * gain


def _norm_mod(x, gain, scale, shift):
    return _rms_norm(x, gain, NORM_EPS) * (1.0 + scale) + shift


def _with_casts(body, n_in, n_cast):
    def kernel(*refs):
        cast_in = refs[n_in:n_in + n_cast]
        out_pos = n_in + n_cast
        cast_out = refs[out_pos + 1:out_pos + 1 + n_cast]
        for src, dst in zip(cast_in, cast_out):
            dst[...] = src[...].astype(_BF16)
        body(*refs[:n_in], refs[out_pos], *refs[out_pos + 1 + n_cast:])

    return kernel


def _cast_plan(casts, n_chunks, chunk_of):
    in_specs, out_specs, out_shapes, args = [], [], [], []
    for w, layer in casts:
        _, r, c = w.shape
        rows = r // n_chunks
        in_specs.append(pl.BlockSpec(
            (None, rows, c), lambda *g, layer=layer: (layer, chunk_of(*g), 0)))
        out_specs.append(pl.BlockSpec((rows, c), lambda *g: (chunk_of(*g), 0)))
        out_shapes.append(jax.ShapeDtypeStruct((r, c), _BF16))
        args.append(w)
    return in_specs, out_specs, out_shapes, args


def _mod_kernel(c_ref, w_ref, b_ref, o_ref):
    cs = _silu(c_ref[...]).astype(_BF16)
    o_ref[...] = _dot(cs, w_ref[...].astype(_BF16)) + b_ref[...]


def _mod_call(c, mod_w, mod_b):
    depth, d, n6 = mod_w.shape
    bsz = c.shape[0]
    tn = 2048
    return pl.pallas_call(
        _mod_kernel,
        grid=(depth, n6 // tn),
        in_specs=[
            pl.BlockSpec((bsz, d), lambda l, j: (0, 0)),
            pl.BlockSpec((None, d, tn), lambda l, j: (l, 0, j)),
            pl.BlockSpec((None, 1, tn), lambda l, j: (l, 0, j)),
        ],
        out_specs=pl.BlockSpec((None, bsz, tn), lambda l, j: (l, 0, j)),
        out_shape=jax.ShapeDtypeStruct((depth, bsz, n6), _F32),
        compiler_params=pltpu.CompilerParams(
            dimension_semantics=("arbitrary", "arbitrary"),
            vmem_limit_bytes=VMEM_LIMIT),
        name="adaln_mod",
    )(c, mod_w, mod_b.reshape(depth, 1, n6))


def _mlp_kernel(x_ref, mod_ref, gain_ref, w1_ref, w2_ref, gf_ref, o_ref, hid_ref, *, final, fc):
    x = x_ref[...]
    m = mod_ref[...]
    h = _norm_mod(x, gain_ref[...], m[4:5], m[3:4]).astype(_BF16)
    d_ff = w1_ref.shape[1]
    for c in range(d_ff // fc):
        t = jnp.maximum(_dot(h, w1_ref[:, c * fc:(c + 1) * fc]), 0.0)
        hid_ref[:, c * fc:(c + 1) * fc] = (t * t).astype(_BF16)
    y = _dot(hid_ref[...], w2_ref[...])
    xn = x + m[5:6] * y
    if final:
        xn = _rms_norm(xn, gf_ref[...], NORM_EPS)
    o_ref[...] = xn


def _mlp_call(x, mod_l, gain, w1, w2, gain_final, *, final, casts=(), tm=1024, fc=1024):
    bsz, t, d = x.shape
    d_ff = w1.shape[1]
    nt = t // tm
    const = lambda b, i: (0, 0)
    c_in, c_out, c_shapes, c_args = _cast_plan(casts, bsz * nt, lambda b, i: b * nt + i)
    in_specs = [
        pl.BlockSpec((None, tm, d), lambda b, i: (b, i, 0)),
        pl.BlockSpec((None, 6, d), lambda b, i: (b, 0, 0)),
        pl.BlockSpec((1, d), const),
        pl.BlockSpec((d, d_ff), const, pipeline_mode=pl.Buffered(1)),
        pl.BlockSpec((d_ff, d), const, pipeline_mode=pl.Buffered(1)),
        pl.BlockSpec((1, d), const),
    ]
    return pl.pallas_call(
        _with_casts(functools.partial(_mlp_kernel, final=final, fc=fc), len(in_specs), len(casts)),
        grid=(bsz, nt),
        in_specs=in_specs + c_in,
        out_specs=[pl.BlockSpec((None, tm, d), lambda b, i: (b, i, 0))] + c_out,
        out_shape=[jax.ShapeDtypeStruct((bsz, t, d), _F32)] + c_shapes,
        scratch_shapes=[pltpu.VMEM((tm, d_ff), _BF16)],
        compiler_params=pltpu.CompilerParams(
            dimension_semantics=("arbitrary", "arbitrary"),
            vmem_limit_bytes=VMEM_LIMIT),
        name="mlp_final" if final else "mlp",
    )(x, mod_l, gain, w1, w2, gain_final, *c_args)


def _shift_rows_one(x, first_row):
    row = lax.broadcasted_iota(jnp.int32, x.shape, 0)
    return jnp.where(row == 0, first_row, pltpu.roll(x, 1, 0))


def _rg_kernel(x_ref, mod_ref, gain_ref, win_ref, cw_ref, cb_ref, wg_ref, ba_ref, bx_ref,
               lam_ref, wout_ref, o_ref, a_ref, u_ref, gate_ref, prev_ref, hc_ref):
    nj, _, d = x_ref.shape
    tb = nj * SUBLANES
    tblk = pl.program_id(1)
    slab = lambda p: slice(p * nj, (p + 1) * nj)

    @pl.when(tblk == 0)
    def _():
        prev_ref[...] = jnp.zeros_like(prev_ref)
        hc_ref[...] = jnp.zeros_like(hc_ref)

    m = mod_ref[...]
    x = jnp.concatenate([x_ref[:, p, :] for p in range(SUBLANES)], axis=0)
    h = _norm_mod(x, gain_ref[...], m[1:2], m[0:1]).astype(_BF16)
    xb = _dot(h, win_ref[:, :d])

    cw = cw_ref[...]
    n_tap = cw.shape[0]
    back = {}
    for p in range(SUBLANES - (n_tap - 1), SUBLANES):
        back[p - SUBLANES] = _shift_rows_one(xb[slab(p), :], prev_ref[p:p + 1, :])
    for p in range(SUBLANES):
        prev_ref[p:p + 1, :] = xb[(p + 1) * nj - 1:(p + 1) * nj, :]
    conv = []
    for p in range(SUBLANES):
        acc = cb_ref[...] + xb[slab(p), :] * cw[n_tap - 1:n_tap, :]
        for k in range(1, n_tap):
            src = xb[slab(p - k), :] if p - k >= 0 else back[p - k]
            acc = acc + src * cw[n_tap - 1 - k:n_tap - k, :]
        conv.append(acc)
    xc = jnp.concatenate(conv, axis=0)

    lam = lam_ref[...]
    nl = -lam
    softplus = jnp.maximum(nl, 0.0) + jnp.log1p(jnp.exp(-jnp.abs(nl)))
    rate = (-LRU_C * LOG2_E) * softplus
    first_row = jnp.logical_and(
        lax.broadcasted_iota(jnp.int32, (tb, HEAD_DIM), 0) == 0, tblk == 0)
    heads = [slice(hd * HEAD_DIM, (hd + 1) * HEAD_DIM) for hd in range(N_HEADS)]
    gps = [_dot(xc[:, sl].astype(_BF16), wg_ref[hd]) for hd, sl in enumerate(heads)]
    for hd, sl in enumerate(heads):
        xh = xc[:, sl]
        gp = gps[hd]
        r = _sigmoid(gp[:, :HEAD_DIM] + ba_ref[:, sl])
        i = _sigmoid(gp[:, HEAD_DIM:] + bx_ref[:, sl])
        a = jnp.exp2(r * rate[:, sl])
        y = 1.0 - a * a
        mult = jnp.where(first_row, 1.0, jnp.where(y > 0.0, y * lax.rsqrt(y), 0.0))
        a_ref[:, sl] = a
        u_ref[:, sl] = mult * (i * xh)
        if hd % 2 == 1:
            cols = slice((hd - 1) * HEAD_DIM, (hd + 1) * HEAD_DIM)
            gate_ref[:, cols] = _gelu_tanh(_dot(h, win_ref[:, d + cols.start:d + cols.stop]))

    hl = u_ref[slab(0), :]
    pr = a_ref[slab(0), :]
    for p in range(1, SUBLANES):
        ap = a_ref[slab(p), :]
        hl = ap * hl + u_ref[slab(p), :]
        pr = ap * pr
        u_ref[slab(p), :] = hl
        a_ref[slab(p), :] = pr
    sub = lax.broadcasted_iota(jnp.int32, (SUBLANES, d), 0)
    carry = hc_ref[...]
    cin = []
    for g in range(nj // SUBLANES):
        rows = slice(g * SUBLANES, (g + 1) * SUBLANES)
        u = hl[rows, :]
        a = pr[rows, :]
        for shift in (1, 2, 4):
            keep = sub >= shift
            u = u + a * jnp.where(keep, pltpu.roll(u, shift, 0), 0.0)
            a = a * jnp.where(keep, pltpu.roll(a, shift, 0), 1.0)
        cout = u + a * carry
        cin.append(jnp.where(sub == 0, carry, pltpu.roll(cout, 1, 0)))
        carry = jnp.broadcast_to(cout[SUBLANES - 1:SUBLANES, :], (SUBLANES, d))
    hc_ref[...] = carry
    cin = jnp.concatenate(cin, axis=0)
    hs = [u_ref[slab(p), :] + a_ref[slab(p), :] * cin for p in range(SUBLANES)]
    z = (jnp.concatenate(hs, axis=0) * gate_ref[...]).astype(_BF16)
    out = x + m[2:3] * _dot(z, wout_ref[...])
    for p in range(SUBLANES):
        o_ref[:, p, :] = out[slab(p), :]


def _rg_call(x, mod_l, gain, w_in, conv_w, conv_b, wg, b_a, b_x, lam, w_out, *, casts=(), tb=512):
    bsz, t, d = x.shape
    nj = tb // SUBLANES
    ntb = t // tb
    const2 = lambda b, i: (0, 0)
    const3 = lambda b, i: (0, 0, 0)
    c_in, c_out, c_shapes, c_args = _cast_plan(casts, bsz * ntb, lambda b, i: b * ntb + i)
    xv = x.reshape(bsz, t // SUBLANES, SUBLANES, d)
    in_specs = [
        pl.BlockSpec((None, nj, SUBLANES, d), lambda b, i: (b, i, 0, 0)),
        pl.BlockSpec((None, 6, d), lambda b, i: (b, 0, 0)),
        pl.BlockSpec((1, d), const2),
        pl.BlockSpec(w_in.shape, const2, pipeline_mode=pl.Buffered(1)),
        pl.BlockSpec(conv_w.shape, const2),
        pl.BlockSpec((1, d), const2),
        pl.BlockSpec(wg.shape, const3, pipeline_mode=pl.Buffered(1)),
        pl.BlockSpec((1, d), const2),
        pl.BlockSpec((1, d), const2),
        pl.BlockSpec((1, d), const2),
        pl.BlockSpec(w_out.shape, const2, pipeline_mode=pl.Buffered(1)),
    ]
    out = pl.pallas_call(
        _with_casts(_rg_kernel, len(in_specs), len(casts)),
        grid=(bsz, ntb),
        in_specs=in_specs + c_in,
        out_specs=[pl.BlockSpec((None, nj, SUBLANES, d), lambda b, i: (b, i, 0, 0))] + c_out,
        out_shape=[jax.ShapeDtypeStruct(xv.shape, _F32)] + c_shapes,
        scratch_shapes=[
            pltpu.VMEM((tb, d), _F32),
            pltpu.VMEM((tb, d), _F32),
            pltpu.VMEM((tb, d), _F32),
            pltpu.VMEM((SUBLANES, d), _F32),
            pltpu.VMEM((SUBLANES, d), _F32),
        ],
        compiler_params=pltpu.CompilerParams(
            dimension_semantics=("arbitrary", "arbitrary"),
            vmem_limit_bytes=VMEM_LIMIT),
        name="rglru",
    )(xv, mod_l, gain, w_in, conv_w, conv_b, wg, b_a, b_x, lam, w_out, *c_args)
    return [out[0].reshape(bsz, t, d)] + list(out[1:])


def _block_row_bcast(b, block, row):
    n, d = b.shape
    parts = []
    for s in range(0, n, block):
        parts.append(jnp.broadcast_to(b[s + row:s + row + 1, :], (block, d)))
    return jnp.concatenate(parts, axis=0)


def _hg_lower_bound(lbp_ref, layer):
    lbp = lbp_ref[...]
    depth = lbp.shape[0]
    mx = lbp[0:1]
    for j in range(1, depth):
        mx = jnp.maximum(mx, lbp[j:j + 1])
    es = [jnp.exp(lbp[j:j + 1] - mx) for j in range(depth)]
    tot = es[0]
    for j in range(1, depth):
        tot = tot + es[j]
    num = es[1]
    for j in range(2, layer + 1):
        num = num + es[j]
    return num / tot


class _HgProducer:
    def __init__(self, x_ref, mod_ref, gain_ref, win_ref, lbp_ref, slot, h_ref, layer):
        self.x_ref, self.win_ref, self.h_ref = x_ref, win_ref, h_ref
        self.q_ref, self.k_ref, self.b_ref, self.v_ref, self.g_ref = slot
        self.tb, self.d = x_ref.shape
        self.lb = _hg_lower_bound(lbp_ref, layer)
        m = mod_ref[...]
        self.scale = gain_ref[...] * (1.0 + m[1:2])
        self.shift = m[0:1]
        cc = HG_CHUNK
        row = lax.broadcasted_iota(jnp.int32, (cc, cc), 0)
        col = lax.broadcasted_iota(jnp.int32, (cc, cc), 1)
        tril = jnp.where(row >= col, 1.0, 0.0).astype(_BF16)
        self.tril2 = jnp.concatenate([tril, tril], axis=1)

    def norm(self):
        x = self.x_ref[...]
        ms = jnp.mean(x * x, axis=-1, keepdims=True)
        self.h_ref[...] = (x * lax.rsqrt(ms + NORM_EPS) * self.scale + self.shift).astype(_BF16)

    def _proj(self, part, rows, cols):
        w = self.win_ref[:, part * self.d + cols.start:part * self.d + cols.stop]
        return _dot(self.h_ref[rows, :], w)

    def query(self, rows, cols):
        self.q_ref[rows, cols] = _silu(self._proj(0, rows, cols))

    def forget(self, rows, cols):
        lb = self.lb[:, cols]
        fg = lb + (1.0 - lb) * _sigmoid(self._proj(1, rows, cols))
        self.k_ref[rows, cols] = 1.0 - fg
        lf = jnp.log2(fg)
        p0 = lf.astype(_BF16)
        p1 = (lf - p0.astype(_F32)).astype(_BF16)
        for r in range(0, rows.stop - rows.start, HG_CHUNK):
            self.b_ref[rows.start + r:rows.start + r + HG_CHUNK, cols] = _dot(
                self.tril2, jnp.concatenate([p0[r:r + HG_CHUNK], p1[r:r + HG_CHUNK]], axis=0))

    def value(self, rows, cols):
        self.v_ref[rows, cols] = self._proj(2, rows, cols).astype(_BF16)

    def gate(self, rows, cols):
        self.g_ref[rows, cols] = _silu(self._proj(3, rows, cols))


class _HgConsumer:
    def __init__(self, x_ref, mod_ref, gn_ref, wout_ref, slot, st_ref, z_ref, o_ref):
        self.x_ref, self.wout_ref, self.st_ref = x_ref, wout_ref, st_ref
        self.z_ref, self.o_ref = z_ref, o_ref
        self.q_ref, self.k_ref, self.b_ref, self.v_ref, self.g_ref = slot
        self.res_gate = mod_ref[2:3, :]
        self.gn = gn_ref[...]
        cc = HG_CHUNK
        row = lax.broadcasted_iota(jnp.int32, (cc, cc), 0)
        col = lax.broadcasted_iota(jnp.int32, (cc, cc), 1)
        self.levels = []
        s = cc // 2
        while s >= HG_DIAG:
            same = (row // (2 * s)) == (col // (2 * s))
            self.levels.append((s, same & ((row % (2 * s)) >= s) & ((col % (2 * s)) < s)))
            s //= 2
        self.diag_mask = ((row // HG_DIAG) == (col // HG_DIAG)) & (col <= row)
        self.outs = None

    def prep(self, rows):
        cc = HG_CHUNK
        q = self.q_ref[rows, :]
        k = self.k_ref[rows, :]
        b = self.b_ref[rows, :]
        b_last = b[cc - 1:cc, :]
        self.v = self.v_ref[rows, :]
        self.qe = (q * jnp.exp2(b)).astype(_BF16)
        self.ke = (k * jnp.exp2(b_last - b)).astype(_BF16)
        self.dec = jnp.exp2(b_last)
        self.facs = []
        for s, _ in self.levels:
            ref = _block_row_bcast(b, 2 * s, s)
            self.facs.append(((q * jnp.exp2(b - ref)).astype(_BF16),
                              (k * jnp.exp2(ref - b)).astype(_BF16)))
        ref = _block_row_bcast(b, HG_DIAG, HG_DIAG // 2)
        self.qd = (q * jnp.exp2(b - ref)).astype(_BF16)
        self.kd = (k * jnp.exp2(ref - b)).astype(_BF16)
        self.outs = []

    def scores(self, hd):
        sl = slice(hd * HEAD_DIM, (hd + 1) * HEAD_DIM)
        att = jnp.where(self.diag_mask, lax.dot_general(
            self.qd[:, sl], self.kd[:, sl], _NT, preferred_element_type=_F32), 0.0)
        for (s, mask), (qf, kf) in zip(self.levels, self.facs):
            att = jnp.where(mask, lax.dot_general(
                qf[:, sl], kf[:, sl], _NT, preferred_element_type=_F32), att)
        return att.astype(_BF16)

    def head(self, hd, att):
        sl = slice(hd * HEAD_DIM, (hd + 1) * HEAD_DIM)
        st = self.st_ref[hd]
        o = _dot(att, self.v[:, sl]) + lax.dot_general(
            self.qe[:, sl], st.astype(_BF16), _NT, preferred_element_type=_F32)
        self.st_ref[hd] = st * self.dec[:, sl] + lax.dot_general(
            self.v[:, sl], self.ke[:, sl], _TN, preferred_element_type=_F32)
        self.outs.append(_rms_norm(o, self.gn, GNORM_EPS))

    def chunk(self, rows):
        self.prep(rows)
        atts = [self.scores(hd) for hd in range(N_HEADS)]
        for hd in range(N_HEADS):
            self.head(hd, atts[hd])
        self.finish(rows)

    def finish(self, rows):
        self.z_ref[rows, :] = (
            jnp.concatenate(self.outs, axis=1) * self.g_ref[rows, :]).astype(_BF16)

    def out_proj(self, rows, cols):
        y = _dot(self.z_ref[rows, :], self.wout_ref[:, cols])
        self.o_ref[rows, cols] = self.x_ref[rows, cols] + self.res_gate[:, cols] * y


def _hg_kernel(xp_ref, modp_ref, xc_ref, modc_ref, gain_ref, win_ref, lbp_ref, gn_ref, wout_ref,
               o_ref, *scratch, layer, nt_blocks):
    slots = (scratch[0:5], scratch[5:10])
    st_ref, h_ref, z_ref = scratch[10:13]
    tb, d = xp_ref.shape
    s = pl.program_id(0)

    @pl.when(s == 0)
    def _():
        for r in slots[1]:
            r[...] = jnp.zeros_like(r)

    @pl.when(jnp.logical_or(s == 0, (s - 1) % nt_blocks == 0))
    def _():
        st_ref[...] = jnp.zeros_like(st_ref)

    def step(p_slot, c_slot):
        prod = _HgProducer(xp_ref, modp_ref, gain_ref, win_ref, lbp_ref, p_slot, h_ref, layer)
        cons = _HgConsumer(xc_ref, modc_ref, gn_ref, wout_ref, c_slot, st_ref, z_ref, o_ref)
        block, full = slice(0, tb), slice(0, d)
        prod.norm()
        for piece in (prod.query, prod.forget, prod.value, prod.gate):
            piece(block, full)
        for c in range(tb // HG_CHUNK):
            cons.chunk(slice(c * HG_CHUNK, (c + 1) * HG_CHUNK))
        cons.out_proj(block, full)

    @pl.when(s % 2 == 0)
    def _():
        step(slots[0], slots[1])

    @pl.when(s % 2 == 1)
    def _():
        step(slots[1], slots[0])


def _hg_call(x, mod_l, gain, w_in, lbp, gnorm, w_out, *, layer, casts=(), tb=512):
    bsz, t, d = x.shape
    ntb = t // tb
    n_blocks = bsz * ntb
    const2 = lambda s: (0, 0)
    p_blk = lambda s: jnp.minimum(s, n_blocks - 1)
    c_blk = lambda s: jnp.maximum(s - 1, 0)
    x_p = lambda s: (p_blk(s) // ntb, p_blk(s) % ntb, 0)
    x_c = lambda s: (c_blk(s) // ntb, c_blk(s) % ntb, 0)
    c_in, c_out, c_shapes, c_args = _cast_plan(casts, n_blocks, p_blk)
    slot = [pltpu.VMEM((tb, d), _F32), pltpu.VMEM((tb, d), _F32), pltpu.VMEM((tb, d), _F32),
            pltpu.VMEM((tb, d), _BF16), pltpu.VMEM((tb, d), _F32)]
    in_specs = [
        pl.BlockSpec((None, tb, d), x_p),
        pl.BlockSpec((None, 6, d), lambda s: (p_blk(s) // ntb, 0, 0)),
        pl.BlockSpec((None, tb, d), x_c),
        pl.BlockSpec((None, 6, d), lambda s: (c_blk(s) // ntb, 0, 0)),
        pl.BlockSpec((1, d), const2),
        pl.BlockSpec(w_in.shape, const2, pipeline_mode=pl.Buffered(1)),
        pl.BlockSpec(lbp.shape, const2),
        pl.BlockSpec((1, HEAD_DIM), const2),
        pl.BlockSpec(w_out.shape, const2, pipeline_mode=pl.Buffered(1)),
    ]
    return pl.pallas_call(
        _with_casts(functools.partial(_hg_kernel, layer=layer, nt_blocks=ntb),
                    len(in_specs), len(casts)),
        grid=(n_blocks + 1,),
        in_specs=in_specs + c_in,
        out_specs=[pl.BlockSpec((None, tb, d), x_c)] + c_out,
        out_shape=[jax.ShapeDtypeStruct((bsz, t, d), _F32)] + c_shapes,
        scratch_shapes=slot + slot + [
            pltpu.VMEM((N_HEADS, HEAD_DIM, HEAD_DIM), _F32),
            pltpu.VMEM((tb, d), _BF16),
            pltpu.VMEM((tb, d), _BF16)],
        compiler_params=pltpu.CompilerParams(
            dimension_semantics=("arbitrary",),
            vmem_limit_bytes=VMEM_LIMIT),
        name="hgrn2",
    )(x, mod_l, x, mod_l, gain, w_in, lbp, gnorm, w_out, *c_args)


def kernel(x, c, mod_w, mod_b, norm_mix, norm_mlp, norm_final, rg_w_in, rg_conv_w, rg_conv_b,
           rg_w_a, rg_b_a, rg_w_x, rg_b_x, rg_lambda, rg_w_out, hg_w_in, hg_lower_bounds,
           hg_gnorm, hg_w_out, mlp_w1, mlp_w2):
    depth = mod_w.shape[0]
    bsz, t, d = x.shape
    mod = _mod_call(c, mod_w, mod_b).reshape(depth, bsz, 6, d)
    row = lambda p: p.reshape(1, -1)

    def mixer_weights(layer):
        j = layer // 2
        return [(rg_w_in, j), (rg_w_out, j)] if layer % 2 == 0 else [(hg_w_in, j), (hg_w_out, j)]

    mix_w = [rg_w_in[0].astype(_BF16), rg_w_out[0].astype(_BF16)]
    for layer in range(depth):
        j = layer // 2
        mlp_casts = [(mlp_w1, layer), (mlp_w2, layer)]
        if layer % 2 == 0:
            wg = jnp.concatenate([rg_w_a[j], rg_w_x[j]], axis=-1).astype(_BF16)
            x, w1, w2 = _rg_call(x, mod[layer], row(norm_mix[layer]), mix_w[0], rg_conv_w[j],
                                 row(rg_conv_b[j]), wg, row(rg_b_a[j]), row(rg_b_x[j]),
                                 row(rg_lambda[j]), mix_w[1], casts=mlp_casts)
        else:
            x, w1, w2 = _hg_call(x, mod[layer], row(norm_mix[layer]), mix_w[0], hg_lower_bounds,
                                 row(hg_gnorm[j]), mix_w[1], layer=layer, casts=mlp_casts)
        final = layer == depth - 1
        x, *mix_w = _mlp_call(x, mod[layer], row(norm_mlp[layer]), w1, w2, row(norm_final),
                              final=final, casts=[] if final else mixer_weights(layer + 1))
    return x
```
